```python
import math
import jax, jax.numpy as jnp
from jax import lax
import numpy as np

D_MODEL = 1024
BATCH = 8
SEQ = 4096
DEPTH = 4

GRID_W = 64
CTX_LEN = 256
N_MIXERS = 3
HY_ORDER = 2
HY_FILTER_WIDTH = 64
HY_EMB_DIM = 33
HY_INNER_MLPS = 2
HY_DECAY_TARGET = 1e-2
HY_SHORT_DECAY_PCT = 0.3
HY_LONG_DECAY_PCT = 1.5
RW_HEAD_DIM = 64
RW_HEADS = D_MODEL // RW_HEAD_DIM
RW_DECAY_LORA = max(32, int(round(1.8 * D_MODEL ** 0.5 / 32)) * 32)
RW_A_LORA = max(32, int(round(1.8 * D_MODEL ** 0.5 / 32)) * 32)
RW_GATE_LORA = max(32, int(round(0.6 * D_MODEL ** 0.8 / 32)) * 32)
RW_GN_EPS = 64e-5
FN_GROUPS = 8
FN_GROUP_DIM = D_MODEL // FN_GROUPS
N_EXPERTS = 16
EC_CAPACITY = 2
D_FF_EXPERT = 2 * D_MODEL
LN_EPS = 1e-5
ADALN_EPS = 1e-6
DEEPNORM_ALPHA = (2 * DEPTH) ** 0.25
DEEPNORM_BETA = (8 * DEPTH) ** -0.25

kernel_name = "hybrid_hyena_rwkv7_fnet_ecmoe_deepnorm"

F32 = jnp.float32


def _layer_norm(x, eps):
    xf = x.astype(F32)
    mu = jnp.mean(xf, -1, keepdims=True)
    var = jnp.mean(jnp.square(xf - mu), -1, keepdims=True)
    return (xf - mu) * lax.rsqrt(var + eps)


def _ln_affine(x, g, b):
    return (_layer_norm(x, LN_EPS) * g.astype(F32) + b.astype(F32)).astype(x.dtype)


def _modulate(x, shift, scale):
    return (_layer_norm(x, ADALN_EPS) * (1.0 + scale.astype(F32)) + shift.astype(F32)).astype(x.dtype)


def _grid_pos_embed(rows, dim):
    r_idx = jnp.repeat(jnp.arange(rows, dtype=F32), GRID_W)
    c_idx = jnp.tile(jnp.arange(GRID_W, dtype=F32), rows)
    quarter = dim // 4
    omega = 1.0 / (10000.0 ** (jnp.arange(quarter, dtype=F32) / quarter))
    def emb(p):
        a = p[:, None] * omega[None, :]
        return jnp.concatenate([jnp.sin(a), jnp.cos(a)], -1)
    return jnp.concatenate([emb(r_idx), emb(c_idx)], -1)


def _centred_conv3(u, w, b):
    up = jnp.pad(u, ((0, 0), (1, 1), (0, 0)))
    return up[:, :-2] * w[0] + up[:, 1:-1] * w[1] + up[:, 2:] * w[2] + b


def _hyena_filters(L, f_w1, f_b1, f_freq, f_w2, f_b2, f_w3):
    pos = jnp.arange(L, dtype=F32)
    t01 = pos / max(L - 1, 1)
    bands = (HY_EMB_DIM - 1) // 2
    f = jnp.linspace(1e-4, bands - 1, bands, dtype=F32)
    ang = f[None, :] * (2.0 * math.pi * pos / L)[:, None]
    z = jnp.concatenate([t01[:, None], jnp.cos(ang), -jnp.sin(ang)], -1)
    freq = f_freq.astype(F32)
    a = jnp.sin(freq * (z @ f_w1.astype(F32) + f_b1.astype(F32)))
    for n in range(HY_INNER_MLPS):
        a = jnp.sin(freq * (a @ f_w2[n].astype(F32) + f_b2[n].astype(F32)))
    h = (a @ f_w3.astype(F32)).reshape(L, 2, HY_ORDER, D_MODEL)
    max_decay = math.log(HY_DECAY_TARGET) / HY_SHORT_DECAY_PCT
    min_decay = math.log(HY_DECAY_TARGET) / HY_LONG_DECAY_PCT
    deltas = jnp.linspace(min_decay, max_decay, D_MODEL, dtype=F32)
    window = jnp.exp(-t01[:, None] * jnp.abs(deltas)[None, :])
    h = h * window[:, None, None, :]
    two_sided = jnp.concatenate([h[:, 0], jnp.zeros((1, HY_ORDER, D_MODEL), F32), h[:0:-1, 1]], 0)
    two_sided = two_sided / jnp.sum(jnp.abs(two_sided), 0, keepdims=True)
    return jnp.fft.rfft(two_sided, axis=0)


def _hyena(h, w_in, b_in, conv_w, conv_b, f_w1, f_b1, f_freq, f_w2, f_b2, f_w3, f_bias, w_out, b_out):
    _, L, _ = h.shape
    u = _centred_conv3(h @ w_in + b_in, conv_w, conv_b)
    v, x1, x2 = jnp.split(u, 3, -1)
    kf = _hyena_filters(L, f_w1, f_b1, f_freq, f_w2, f_b2, f_w3)
    def long_conv(zin, o):
        zf = zin.astype(F32)
        y = jnp.fft.irfft(jnp.fft.rfft(zf, n=2 * L, axis=1) * kf[None, :, o], n=2 * L, axis=1)[:, :L]
        return (y + zf * f_bias[o].astype(F32)).astype(h.dtype)
    z = x1 * long_conv(v, 0)
    z = x2 * long_conv(z, 1)
    return z @ w_out + b_out


def _heads(t):
    return t.astype(F32).reshape(t.shape[:-1] + (RW_HEADS, RW_HEAD_DIM))


def _token_shift_bidir(h):
    hp = jnp.pad(h, ((0, 0), (1, 1), (0, 0)))
    return 0.5 * (hp[:, :-2] + hp[:, 2:]) - h


def _rwkv_prep(h, mu, wr, wk, wv, w0, w1, w2, a0, a1, a2, k_k, k_a):
    B_, T, _ = h.shape
    xx = _token_shift_bidir(h)
    xr, xw, xk, xv, xa, xg = [h + xx * mu[n] for n in range(6)]
    k = xk @ wk
    lw = jnp.einsum('nbtr,nrc->nbtc', jnp.tanh(jnp.einsum('btc,ncr->nbtr', xw, w1)), w2).astype(F32) + w0[:, None, None, :].astype(F32)
    decay = jnp.exp(-jnp.exp(-jax.nn.softplus(-lw) - 0.5))
    a = jax.nn.sigmoid(jnp.einsum('nbtr,nrc->nbtc', jnp.einsum('btc,ncr->nbtr', xa, a1), a2).astype(F32) + a0[:, None, None, :].astype(F32))
    kk = (k * k_k).astype(F32).reshape(B_, T, RW_HEADS, RW_HEAD_DIM)
    kk = kk * lax.rsqrt(jnp.maximum(jnp.sum(kk * kk, -1, keepdims=True), 1e-24))
    kk = kk.reshape(B_, T, D_MODEL)
    k_dir = k.astype(F32)[None] * (1.0 + (a - 1.0) * k_a.astype(F32))
    return {"r": (xr @ wr).astype(F32), "v": (xv @ wv).astype(F32), "kk": kk, "decay": decay,
            "k": k_dir, "kka": kk[None] * a, "xg": xg}


def _bidir_seq(tc, tl, directional):
    if directional:
        fc, bc, fl, bl = tc[0], tc[1], tl[0], tl[1]
    else:
        fc, bc, fl, bl = tc, tc, tl, tl
    fwd = jnp.concatenate([fc, fl], 1)
    bwd = jnp.concatenate([jnp.flip(bc, 1), jnp.flip(bl, 1)], 1)
    return jnp.stack([fwd, bwd], 0)


def _wkv_scan(r, decay, k, v, kk, kka):
    def step(S, inp):
        r_t, w_t, k_t, v_t, kk_t, kka_t = inp
        sa = jnp.einsum('dbhvk,dbhk->dbhv', S, kk_t)
        S = S * w_t[..., None, :] - sa[..., :, None] * kka_t[..., None, :] + v_t[..., :, None] * k_t[..., None, :]
        return S, jnp.einsum('dbhvk,dbhk->dbhv', S, r_t)
    S0 = jnp.zeros(r.shape[:2] + (RW_HEADS, RW_HEAD_DIM, RW_HEAD_DIM), F32)
    xs = tuple(jnp.moveaxis(t, 2, 0) for t in (r, decay, k, v, kk, kka))
    _, y = lax.scan(step, S0, xs)
    return jnp.moveaxis(y, 0, 2)


def _rwkv_mixer(hc, hl, prep_params, g1, g2, r_k, gn_g, gn_b, w_o, ctx_out):
    pc = _rwkv_prep(hc, *prep_params)
    pl = _rwkv_prep(hl, *prep_params)
    n_ctx = hc.shape[1]
    def seq(name, directional):
        return _heads(_bidir_seq(pc[name], pl[name], directional))
    y = _wkv_scan(seq("r", False), seq("decay", True), seq("k", True), seq("v", False), seq("kk", False), seq("kka", True))
    def finish(y_dir, p, h):
        wkv = y_dir[0] + jnp.flip(y_dir[1], 1)
        m = jnp.mean(wkv, -1, keepdims=True)
        var = jnp.mean(jnp.square(wkv - m), -1, keepdims=True)
        gn = (wkv - m) * lax.rsqrt(var + RW_GN_EPS) * _heads(gn_g) + _heads(gn_b)
        bonus = jnp.sum(_heads(p["r"])[None] * _heads(p["k"]) * _heads(r_k), axis=(0, -1))[..., None] * _heads(p["v"])
        g = jax.nn.sigmoid(p["xg"] @ g1) @ g2
        o = (gn + bonus).reshape(h.shape).astype(h.dtype) * g
        return o @ w_o
    yl = finish(y[:, :, n_ctx:], pl, hl)
    yc = finish(y[:, :, :n_ctx], pc, hc) if ctx_out else None
    return yc, yl


def _fourier(h, w_o, b_o):
    B_, T, _ = h.shape
    hg = h.astype(F32).reshape(B_, T, FN_GROUPS, FN_GROUP_DIM)
    mixed = jnp.real(jnp.fft.fft2(hg, axes=(1, 3), norm="ortho")).reshape(B_, T, D_MODEL)
    return mixed.astype(h.dtype) @ w_o + b_o


def _expert_choice_moe(h, w_router, w1, w3, w2):
    B_, T, _ = h.shape
    cap = EC_CAPACITY * T // N_EXPERTS
    aff = jax.nn.softmax((h @ w_router).astype(F32), -1)
    gate, idx = lax.top_k(jnp.swapaxes(aff, 1, 2), cap)
    bidx = jnp.arange(B_)[:, None, None]
    xe = h[bidx, idx]
    he = jax.nn.silu(jnp.einsum('becd,edf->becf', xe, w1)) * jnp.einsum('becd,edf->becf', xe, w3)
    ye = jnp.einsum('becf,efd->becd', he, w2) * gate[..., None].astype(h.dtype)
    return jnp.zeros_like(h).at[bidx, idx].add(ye)


def setup_inputs(seed: int = 0) -> dict:
    key = jax.random.key(seed)
    keys = iter(jax.random.split(key, 64))
    def nrm(shape, scale):
        return scale * jax.random.normal(next(keys), shape, F32)
    def unif(shape, lo, hi):
        return jax.random.uniform(next(keys), shape, F32, lo, hi)
    D, E, F = D_MODEL, N_EXPERTS, D_FF_EXPERT
    n_a = len(range(0, DEPTH, N_MIXERS))
    n_b = len(range(1, DEPTH, N_MIXERS))
    n_c = len(range(2, DEPTH, N_MIXERS))
    FW = HY_FILTER_WIDTH
    return {
        "x": nrm((BATCH, SEQ, D), 1.0),
        "c": nrm((BATCH, D), 1.0),
        "ctx": nrm((BATCH, CTX_LEN, D), 1.0),
        "c_ctx": nrm((D,), 1.0),
        "mod_w": nrm((DEPTH, D, 6 * D), 0.25 * D ** -0.5),
        "mod_b": nrm((DEPTH, 6 * D), 0.02),
        "ln_g": 1.0 + nrm((DEPTH, 2, D), 0.02),
        "ln_b": nrm((DEPTH, 2, D), 0.02),
        "moe_router": nrm((DEPTH, D, E), D ** -0.5),
        "moe_w1": nrm((DEPTH, E, D, F), D ** -0.5),
        "moe_w3": nrm((DEPTH, E, D, F), D ** -0.5),
        "moe_w2": nrm((DEPTH, E, F, D), DEEPNORM_BETA * F ** -0.5),
        "hy_w_in": nrm((n_a, D, 3 * D), D ** -0.5),
        "hy_b_in": nrm((n_a, 3 * D), 0.02),
        "hy_conv_w": nrm((n_a, 3, 3 * D), 3 ** -0.5),
        "hy_conv_b": nrm((n_a, 3 * D), 0.02),
        "hy_f_w1": nrm((n_a, HY_EMB_DIM, FW), HY_EMB_DIM ** -0.5),
        "hy_f_b1": nrm((n_a, FW), 0.1),
        "hy_f_freq": 1.0 + nrm((n_a, FW), 0.02),
        "hy_f_w2": nrm((n_a, HY_INNER_MLPS, FW, FW), FW ** -0.5),
        "hy_f_b2": nrm((n_a, HY_INNER_MLPS, FW), 0.1),
        "hy_f_w3": nrm((n_a, FW, 2 * HY_ORDER * D), FW ** -0.5),
        "hy_f_bias": nrm((n_a, HY_ORDER, D), 1.0),
        "hy_w_out": nrm((n_a, D, D), DEEPNORM_BETA * D ** -0.5),
        "hy_b_out": nrm((n_a, D), 0.02),
        "rw_mu": unif((n_b, 6, D), 0.0, 1.0),
        "rw_wr": nrm((n_b, D, D), D ** -0.5),
        "rw_wk": nrm((n_b, D, D), D ** -0.5),
        "rw_wv": nrm((n_b, D, D), D ** -0.5),
        "rw_w0": unif((n_b, 2, D), -6.0, 1.0),
        "rw_w1": nrm((n_b, 2, D, RW_DECAY_LORA), D ** -0.5),
        "rw_w2": nrm((n_b, 2, RW_DECAY_LORA, D), 0.1 * RW_DECAY_LORA ** -0.5),
        "rw_a0": nrm((n_b, 2, D), 0.1),
        "rw_a1": nrm((n_b, 2, D, RW_A_LORA), D ** -0.5),
        "rw_a2": nrm((n_b, 2, RW_A_LORA, D), 0.1 * RW_A_LORA ** -0.5),
        "rw_kk": 0.85 + nrm((n_b, D), 0.02),
        "rw_ka": 1.0 + nrm((n_b, D), 0.02),
        "rw_g1": nrm((n_b, D, RW_GATE_LORA), D ** -0.5),
        "rw_g2": nrm((n_b, RW_GATE_LORA, D), RW_GATE_LORA ** -0.5),
        "rw_rk": nrm((n_b, D), 0.1),
        "rw_gn_g": 1.0 + nrm((n_b, D), 0.02),
        "rw_gn_b": nrm((n_b, D), 0.02),
        "rw_wo": nrm((n_b, D, D), DEEPNORM_BETA * D ** -0.5),
        "fn_wo": nrm((n_c, D, D), DEEPNORM_BETA * D ** -0.5),
        "fn_bo": nrm((n_c, D), 0.02),
    }


def reference(x, c, ctx, c_ctx, mod_w, mod_b, ln_g, ln_b, moe_router, moe_w1, moe_w3, moe_w2,
              hy_w_in, hy_b_in, hy_conv_w, hy_conv_b, hy_f_w1, hy_f_b1, hy_f_freq, hy_f_w2, hy_f_b2, hy_f_w3,
              hy_f_bias, hy_w_out, hy_b_out,
              rw_mu, rw_wr, rw_wk, rw_wv, rw_w0, rw_w1, rw_w2, rw_a0, rw_a1, rw_a2, rw_kk, rw_ka,
              rw_g1, rw_g2, rw_rk, rw_gn_g, rw_gn_b, rw_wo,
              fn_wo, fn_bo):
    n_lat = x.shape[1]
    rows = n_lat // GRID_W
    x = x + _grid_pos_embed(rows, D_MODEL).astype(x.dtype)[None]
    xc = ctx
    readers = [i for i in range(DEPTH) if i % N_MIXERS == 1]
    last_reader = readers[-1] if readers else -1
    for i in range(DEPTH):
        kind, j = i % N_MIXERS, i // N_MIXERS
        ctx_in = i <= last_reader
        ctx_full = i < last_reader
        m_l = jnp.split((jax.nn.silu(c) @ mod_w[i] + mod_b[i])[:, None, :], 6, -1)
        hl = _modulate(x, m_l[0], m_l[1])
        if ctx_in:
            m_c = jnp.split((jax.nn.silu(c_ctx) @ mod_w[i] + mod_b[i])[None, None, :], 6, -1)
            hc = _modulate(xc, m_c[0], m_c[1])
        if kind == 0:
            hy = (hy_w_in[j], hy_b_in[j], hy_conv_w[j], hy_conv_b[j], hy_f_w1[j], hy_f_b1[j], hy_f_freq[j],
                  hy_f_w2[j], hy_f_b2[j], hy_f_w3[j], hy_f_bias[j], hy_w_out[j], hy_b_out[j])
            yl = _hyena(hl, *hy)
            yc = _hyena(hc, *hy) if ctx_full else None
        elif kind == 1:
            prep = (rw_mu[j], rw_wr[j], rw_wk[j], rw_wv[j], rw_w0[j], rw_w1[j], rw_w2[j],
                    rw_a0[j], rw_a1[j], rw_a2[j], rw_kk[j], rw_ka[j])
            yc, yl = _rwkv_mixer(hc, hl, prep, rw_g1[j], rw_g2[j], rw_rk[j], rw_gn_g[j], rw_gn_b[j], rw_wo[j], ctx_full)
        else:
            yl = _fourier(hl, fn_wo[j], fn_bo[j])
            yc = _fourier(hc, fn_wo[j], fn_bo[j]) if ctx_full else None
        x = _ln_affine(DEEPNORM_ALPHA * x + (1.0 + m_l[2]) * yl, ln_g[i, 0], ln_b[i, 0])
        hl = _modulate(x, m_l[3], m_l[4])
        x = _ln_affine(DEEPNORM_ALPHA * x + (1.0 + m_l[5]) * _expert_choice_moe(hl, moe_router[i], moe_w1[i], moe_w3[i], moe_w2[i]),
                       ln_g[i, 1], ln_b[i, 1])
        if ctx_full:
            xc = _ln_affine(DEEPNORM_ALPHA * xc + (1.0 + m_c[2]) * yc, ln_g[i, 0], ln_b[i, 0])
            hc = _modulate(xc, m_c[3], m_c[4])
            xc = _ln_affine(DEEPNORM_ALPHA * xc + (1.0 + m_c[5]) * _expert_choice_moe(hc, moe_router[i], moe_w1[i], moe_w3[i], moe_w2[i]),
                            ln_g[i, 1], ln_b[i, 1])
    return x
```

```python
import functools
import math

import jax
import jax.numpy as jnp
from jax import lax
from jax.experimental import pallas as pl
from jax.experimental.pallas import tpu as pltpu

F32 = jnp.float32
HIGHEST = lax.Precision.HIGHEST

GRID_W = 64
N_MIXERS = 3
HY_ORDER = 2
HY_EMB_DIM = 33
HY_INNER_MLPS = 2
HY_DECAY_TARGET = 1e-2
HY_SHORT_DECAY_PCT = 0.3
HY_LONG_DECAY_PCT = 1.5
RW_HEAD_DIM = 64
RW_GN_EPS = 64e-5
FN_GROUPS = 8
N_EXPERTS = 16
EC_CAPACITY = 2
LN_EPS = 1e-5
ADALN_EPS = 1e-6

SCAN_CHUNK = 64
LANES = 128


def _layer_norm(x, eps):
    xf = x.astype(F32)
    mu = jnp.mean(xf, -1, keepdims=True)
    var = jnp.mean(jnp.square(xf - mu), -1, keepdims=True)
    return (xf - mu) * lax.rsqrt(var + eps)


def _ln_affine(x, g, b):
    return (_layer_norm(x, LN_EPS) * g.astype(F32) + b.astype(F32)).astype(x.dtype)


def _modulate(x, shift, scale):
    return (_layer_norm(x, ADALN_EPS) * (1.0 + scale.astype(F32)) + shift.astype(F32)).astype(x.dtype)


def _grid_pos_embed(rows, dim):
    r_idx = jnp.repeat(jnp.arange(rows, dtype=F32), GRID_W)
    c_idx = jnp.tile(jnp.arange(GRID_W, dtype=F32), rows)
    quarter = dim // 4
    omega = 1.0 / (10000.0 ** (jnp.arange(quarter, dtype=F32) / quarter))

    def emb(p):
        a = p[:, None] * omega[None, :]
        return jnp.concatenate([jnp.sin(a), jnp.cos(a)], -1)

    return jnp.concatenate([emb(r_idx), emb(c_idx)], -1)


def _centred_conv3(u, w, b):
    up = jnp.pad(u, ((0, 0), (1, 1), (0, 0)))
    return up[:, :-2] * w[0] + up[:, 1:-1] * w[1] + up[:, 2:] * w[2] + b


def _hyena_filters(L, D, f_w1, f_b1, f_freq, f_w2, f_b2, f_w3):
    pos = jnp.arange(L, dtype=F32)
    t01 = pos / max(L - 1, 1)
    bands = (HY_EMB_DIM - 1) // 2
    f = jnp.linspace(1e-4, bands - 1, bands, dtype=F32)
    ang = f[None, :] * (2.0 * math.pi * pos / L)[:, None]
    z = jnp.concatenate([t01[:, None], jnp.cos(ang), -jnp.sin(ang)], -1)
    freq = f_freq.astype(F32)
    a = jnp.sin(freq * (z @ f_w1.astype(F32) + f_b1.astype(F32)))
    for n in range(HY_INNER_MLPS):
        a = jnp.sin(freq * (a @ f_w2[n].astype(F32) + f_b2[n].astype(F32)))
    h = (a @ f_w3.astype(F32)).reshape(L, 2, HY_ORDER, D)
    max_decay = math.log(HY_DECAY_TARGET) / HY_SHORT_DECAY_PCT
    min_decay = math.log(HY_DECAY_TARGET) / HY_LONG_DECAY_PCT
    deltas = jnp.linspace(min_decay, max_decay, D, dtype=F32)
    window = jnp.exp(-t01[:, None] * jnp.abs(deltas)[None, :])
    h = h * window[:, None, None, :]
    two_sided = jnp.concatenate([h[:, 0], jnp.zeros((1, HY_ORDER, D), F32), h[:0:-1, 1]], 0)
    two_sided = two_sided / jnp.sum(jnp.abs(two_sided), 0, keepdims=True)
    return jnp.fft.rfft(two_sided, axis=0)


def _hyena(h, w_in, b_in, conv_w, conv_b, f_w1, f_b1, f_freq, f_w2, f_b2, f_w3, f_bias, w_out, b_out):
    _, L, D = h.shape
    u = _centred_conv3(h @ w_in + b_in, conv_w, conv_b)
    v, x1, x2 = jnp.split(u, 3, -1)
    kf = _hyena_filters(L, D, f_w1, f_b1, f_freq, f_w2, f_b2, f_w3)

    def long_conv(zin, o):
        zf = zin.astype(F32)
        y = jnp.fft.irfft(jnp.fft.rfft(zf, n=2 * L, axis=1) * kf[None, :, o], n=2 * L, axis=1)[:, :L]
        return (y + zf * f_bias[o].astype(F32)).astype(h.dtype)

    z = x1 * long_conv(v, 0)
    z = x2 * long_conv(z, 1)
    return z @ w_out + b_out


def _token_shift_bidir(h):
    hp = jnp.pad(h, ((0, 0), (1, 1), (0, 0)))
    return 0.5 * (hp[:, :-2] + hp[:, 2:]) - h


def _rwkv_prep(h, mu, wr, wk, wv, w0, w1, w2, a0, a1, a2, k_k, k_a):
    B_, T, D = h.shape
    H = D // RW_HEAD_DIM
    xx = _token_shift_bidir(h)
    xr, xw, xk, xv, xa, xg = [h + xx * mu[n] for n in range(6)]
    k = xk @ wk
    lw = jnp.einsum('nbtr,nrc->nbtc', jnp.tanh(jnp.einsum('btc,ncr->nbtr', xw, w1)), w2).astype(F32) + w0[:, None, None, :].astype(F32)
    log_decay = -math.exp(-0.5) * jax.nn.sigmoid(lw)
    a = jax.nn.sigmoid(jnp.einsum('nbtr,nrc->nbtc', jnp.einsum('btc,ncr->nbtr', xa, a1), a2).astype(F32) + a0[:, None, None, :].astype(F32))
    kk = (k * k_k).astype(F32).reshape(B_, T, H, RW_HEAD_DIM)
    kk = kk * lax.rsqrt(jnp.maximum(jnp.sum(kk * kk, -1, keepdims=True), 1e-24))
    kk = kk.reshape(B_, T, D)
    k_dir = k.astype(F32)[None] * (1.0 + (a - 1.0) * k_a.astype(F32))
    return {"r": (xr @ wr).astype(F32), "v": (xv @ wv).astype(F32), "kk": kk, "log_decay": log_decay,
            "k": k_dir, "kka": kk[None] * a, "xg": xg}


def _scan_kernel(r_ref, v_ref, kk_ref, lw_ref, k_ref, kka_ref, y_ref, s_ref):
    C = SCAN_CHUNK
    hd = RW_HEAD_DIM
    d = pl.program_id(0)

    @pl.when(pl.program_id(2) == 0)
    def _():
        s_ref[...] = jnp.zeros_like(s_ref)

    sign = 1 - 2 * d
    row = lax.broadcasted_iota(jnp.int32, (C, C), 0)
    col = lax.broadcasted_iota(jnp.int32, (C, C), 1)
    diff = (row - col) * sign
    incl = diff >= 0
    strict = diff > 0
    incl_f = incl.astype(F32)
    eye = (row == col).astype(F32)

    def dot(a, b):
        return jnp.dot(a, b, precision=HIGHEST, preferred_element_type=F32)

    def dot_nt(a, b):
        return lax.dot_general(a, b, (((1,), (1,)), ((), ())), precision=HIGHEST, preferred_element_type=F32)

    def dot_tn(a, b):
        return lax.dot_general(a, b, (((0,), (0,)), ((), ())), precision=HIGHEST, preferred_element_type=F32)

    def head_pair(p, carry):
        sl = pl.ds(pl.multiple_of(p * LANES, LANES), LANES)
        lw = lw_ref[0, 0, :, sl]
        cum = dot(incl_f, lw)
        g_in = jnp.exp(cum)
        g_ex = jnp.exp(cum - lw)
        g_inv = jnp.exp(-cum)
        g_end = jnp.exp(jnp.sum(lw, axis=0, keepdims=True))
        a_t = -kk_ref[0, :, sl] * g_ex
        b_t = kka_ref[0, 0, :, sl] * g_inv
        k_t = k_ref[0, 0, :, sl] * g_inv
        r_t = r_ref[0, :, sl] * g_in
        v = v_ref[0, :, sl]
        ys = []
        for hh in range(LANES // hd):
            ls = slice(hh * hd, (hh + 1) * hd)
            a_h, b_h, k_h, r_h, v_h = a_t[:, ls], b_t[:, ls], k_t[:, ls], r_t[:, ls], v[:, ls]
            bk = jnp.concatenate([b_h, k_h], axis=0)
            sc = dot_nt(jnp.concatenate([a_h, r_h], axis=0), bk)
            a_ab = jnp.where(strict, sc[:C, :C], 0.0)
            a_ak = jnp.where(strict, sc[:C, C:], 0.0)
            m_rb = jnp.where(incl, sc[C:, :C], 0.0)
            m_rk = jnp.where(incl, sc[C:, C:], 0.0)
            inv = eye + a_ab
            apow = a_ab
            n = 2
            while n < C:
                apow = dot(apow, apow)
                inv = inv + dot(inv, apow)
                n *= 2
            av = dot(jnp.concatenate([a_ak, m_rk], axis=0), v_h)
            wu = dot(inv, jnp.concatenate([a_h, av[:C]], axis=1))
            s0 = s_ref[2 * p + hh]
            ws = dot_nt(jnp.concatenate([wu[:, :hd], r_h], axis=0), s0)
            u = ws[:C] + wu[:, hd:]
            ys.append(ws[C:] + dot(m_rb, u) + av[C:])
            s_new = s0 + dot_tn(jnp.concatenate([u, v_h], axis=0), bk)
            s_ref[2 * p + hh] = s_new * g_end[:, ls]
        y_ref[0, 0, :, sl] = jnp.concatenate(ys, axis=1)
        return carry

    lax.fori_loop(0, lw_ref.shape[-1] // LANES, head_pair, 0)


def _wkv_scan(r, v, kk, log_decay, k_dir, kka, n_ctx):
    B_, T, D = r.shape
    C = SCAN_CHUNK
    assert n_ctx % C == 0 and T % C == 0 and D % LANES == 0
    nc_ctx, nc = n_ctx // C, T // C

    def chunk(d, j):
        back = jnp.where(j < nc_ctx, nc_ctx - 1 - j, nc + nc_ctx - 1 - j)
        return jnp.where(d == 0, j, back)

    shared = pl.BlockSpec((1, C, D), lambda d, b, j: (b, chunk(d, j), 0))
    directed = pl.BlockSpec((1, 1, C, D), lambda d, b, j: (d, b, chunk(d, j), 0))
    return pl.pallas_call(
        _scan_kernel,
        grid=(2, B_, nc),
        in_specs=[shared, shared, shared, directed, directed, directed],
        out_specs=directed,
        out_shape=jax.ShapeDtypeStruct((2, B_, T, D), F32),
        scratch_shapes=[pltpu.VMEM((D // RW_HEAD_DIM, RW_HEAD_DIM, RW_HEAD_DIM), F32)],
        compiler_params=pltpu.CompilerParams(dimension_semantics=("parallel", "parallel", "arbitrary")),
        name="wkv_scan",
    )(r, v, kk, log_decay, k_dir, kka)


def _heads(t):
    return t.astype(F32).reshape(t.shape[:-1] + (t.shape[-1] // RW_HEAD_DIM, RW_HEAD_DIM))


def _rwkv_mixer(hc, hl, prep_params, g1, g2, r_k, gn_g, gn_b, w_o):
    pc = _rwkv_prep(hc, *prep_params)
    pl_ = _rwkv_prep(hl, *prep_params)
    n_ctx = hc.shape[1]

    def seq(name):
        return jnp.concatenate([pc[name], pl_[name]], axis=-2)

    y = _wkv_scan(seq("r"), seq("v"), seq("kk"), seq("log_decay"), seq("k"), seq("kka"), n_ctx)
    p, h = pl_, hl
    wkv = _heads(y[0, :, n_ctx:] + y[1, :, n_ctx:])
    m = jnp.mean(wkv, -1, keepdims=True)
    var = jnp.mean(jnp.square(wkv - m), -1, keepdims=True)
    gn = (wkv - m) * lax.rsqrt(var + RW_GN_EPS) * _heads(gn_g) + _heads(gn_b)
    bonus = jnp.sum(_heads(p["r"])[None] * _heads(p["k"]) * _heads(r_k), axis=(0, -1))[..., None] * _heads(p["v"])
    g = jax.nn.sigmoid(p["xg"] @ g1) @ g2
    o = (gn + bonus).reshape(h.shape).astype(h.dtype) * g
    return o @ w_o


def _fourier(h, w_o, b_o):
    B_, T, D = h.shape
    hg = h.astype(F32).reshape(B_, T, FN_GROUPS, D // FN_GROUPS)
    mixed = jnp.real(jnp.fft.fft2(hg, axes=(1, 3), norm="ortho")).reshape(B_, T, D)
    return mixed.astype(h.dtype) @ w_o + b_o


def _expert_choice_moe(h, w_router, w1, w3, w2):
    B_, T, _ = h.shape
    cap = EC_CAPACITY * T // N_EXPERTS
    aff = jax.nn.softmax((h @ w_router).astype(F32), -1)
    gate, idx = lax.top_k(jnp.swapaxes(aff, 1, 2), cap)
    bidx = jnp.arange(B_)[:, None, None]
    xe = h[bidx, idx]
    he = jax.nn.silu(jnp.einsum('becd,edf->becf', xe, w1)) * jnp.einsum('becd,edf->becf', xe, w3)
    ye = jnp.einsum('becf,efd->becd', he, w2) * gate[..., None].astype(h.dtype)
    return jnp.zeros_like(h).at[bidx, idx].add(ye)


def kernel(x, c, ctx, c_ctx, mod_w, mod_b, ln_g, ln_b, moe_router, moe_w1, moe_w3, moe_w2, hy_w_in, hy_b_in, hy_conv_w, hy_conv_b, hy_f_w1, hy_f_b1, hy_f_freq, hy_f_w2, hy_f_b2, hy_f_w3, hy_f_bias, hy_w_out, hy_b_out, rw_mu, rw_wr, rw_wk, rw_wv, rw_w0, rw_w1, rw_w2, rw_a0, rw_a1, rw_a2, rw_kk, rw_ka, rw_g1, rw_g2, rw_rk, rw_gn_g, rw_gn_b, rw_wo, fn_wo, fn_bo):
    depth = mod_w.shape[0]
    D = x.shape[-1]
    alpha = (2 * depth) ** 0.25
    n_lat = x.shape[1]
    x = x + _grid_pos_embed(n_lat // GRID_W, D).astype(x.dtype)[None]
    xc = ctx
    readers = [i for i in range(depth) if i % N_MIXERS == 1]
    last_reader = readers[-1] if readers else -1
    for i in range(depth):
        kind, j = i % N_MIXERS, i // N_MIXERS
        ctx_in = i <= last_reader
        ctx_full = i < last_reader
        m_l = jnp.split((jax.nn.silu(c) @ mod_w[i] + mod_b[i])[:, None, :], 6, -1)
        hl = _modulate(x, m_l[0], m_l[1])
        if ctx_in:
            m_c = jnp.split((jax.nn.silu(c_ctx) @ mod_w[i] + mod_b[i])[None, None, :], 6, -1)
            hc = _modulate(xc, m_c[0], m_c[1])
        if kind == 0:
            hy = (hy_w_in[j], hy_b_in[j], hy_conv_w[j], hy_conv_b[j], hy_f_w1[j], hy_f_b1[j], hy_f_freq[j],
                  hy_f_w2[j], hy_f_b2[j], hy_f_w3[j], hy_f_bias[j], hy_w_out[j], hy_b_out[j])
            yl = _hyena(hl, *hy)
            yc = _hyena(hc, *hy) if ctx_full else None
        elif kind == 1:
            prep = (rw_mu[j], rw_wr[j], rw_wk[j], rw_wv[j], rw_w0[j], rw_w1[j], rw_w2[j],
                    rw_a0[j], rw_a1[j], rw_a2[j], rw_kk[j], rw_ka[j])
            assert not ctx_full
            yc = None
            yl = _rwkv_mixer(hc, hl, prep, rw_g1[j], rw_g2[j], rw_rk[j], rw_gn_g[j], rw_gn_b[j], rw_wo[j])
        else:
            yl = _fourier(hl, fn_wo[j], fn_bo[j])
            yc = _fourier(hc, fn_wo[j], fn_bo[j]) if ctx_full else None
        x = _ln_affine(alpha * x + (1.0 + m_l[2]) * yl, ln_g[i, 0], ln_b[i, 0])
        hl = _modulate(x, m_l[3], m_l[4])
        x = _ln_affine(alpha * x + (1.0 + m_l[5]) * _expert_choice_moe(hl, moe_router[i], moe_w1[i], moe_w3[i], moe_w2[i]),
                       ln_g[i, 1], ln_b[i, 1])
        if ctx_full:
            xc = _ln_affine(alpha * xc + (1.0 + m_c[2]) * yc, ln_g[i, 0], ln_b[i, 0])
            hc = _modulate(xc, m_c[3], m_c[4])
            xc = _ln_affine(alpha * xc + (1.0 + m_c[5]) * _expert_choice_moe(hc, moe_router[i], moe_w1[i], moe_w3[i], moe_w2[i]),
                            ln_g[i, 1], ln_b[i, 1])
    return x
```

```python
import functools
import math

import ml_dtypes
import numpy as np

import jax
import jax.numpy as jnp
from jax import lax
from jax.experimental import pallas as pl
from jax.experimental.pallas import tpu as pltpu

F32 = jnp.float32
BF16 = jnp.bfloat16

GRID_W = 64
N_MIXERS = 3
HY_ORDER = 2
HY_EMB_DIM = 33
HY_INNER_MLPS = 2
HY_DECAY_TARGET = 1e-2
HY_SHORT_DECAY_PCT = 0.3
HY_LONG_DECAY_PCT = 1.5
RW_HEAD_DIM = 64
RW_GN_EPS = 64e-5
FN_GROUPS = 8
N_EXPERTS = 16
EC_CAPACITY = 2
LN_EPS = 1e-5
ADALN_EPS = 1e-6

SUBLANES = 8
LANE_TILE = 128
SCAN_CHUNK = 64
SCAN_LANES = 512
DFT_MINOR = 128
DFT_DC = 1024
FN_ROWS = 512

_NT = (((1,), (1,)), ((), ()))
_TN = (((0,), (0,)), ((), ()))


def _split3(x):
    hi = x.astype(BF16)
    r1 = x - hi.astype(F32)
    mid = r1.astype(BF16)
    lo = (r1 - mid.astype(F32)).astype(BF16)
    return hi, mid, lo


def _split2(x):
    hi = x.astype(BF16)
    return hi, (x - hi.astype(F32)).astype(BF16)


def _dot3(a, b, dims=(((1,), (0,)), ((), ()))):
    a_hi, a_lo = _split2(a)
    b_hi, b_lo = _split2(b)
    dg = functools.partial(lax.dot_general, dimension_numbers=dims, preferred_element_type=F32)
    return dg(a_hi, b_hi) + (dg(a_hi, b_lo) + dg(a_lo, b_hi))


def _lhs_parts(m):
    hi = m.astype(ml_dtypes.bfloat16)
    lo = (m - hi.astype(np.float64)).astype(ml_dtypes.bfloat16)
    return jnp.asarray(np.concatenate([hi, hi, lo], axis=1))


def _rhs_parts(x):
    hi, lo = _split2(x)
    return jnp.concatenate([hi, lo, hi], axis=0)


def _rhs_parts_t(x):
    hi, lo = _split2(x)
    return jnp.concatenate([hi, lo, hi], axis=1)


def _lane_tiled(t, width):
    return jnp.concatenate([t] * (width // t.shape[-1]), axis=-1)


def _layer_norm(x, eps):
    xf = x.astype(F32)
    mu = jnp.mean(xf, -1, keepdims=True)
    var = jnp.mean(jnp.square(xf - mu), -1, keepdims=True)
    return (xf - mu) * lax.rsqrt(var + eps)


def _ln_affine(x, g, b):
    return (_layer_norm(x, LN_EPS) * g.astype(F32) + b.astype(F32)).astype(x.dtype)


def _modulate(x, shift, scale):
    return (_layer_norm(x, ADALN_EPS) * (1.0 + scale.astype(F32)) + shift.astype(F32)).astype(x.dtype)


def _grid_pos_embed(rows, dim):
    r_idx = jnp.repeat(jnp.arange(rows, dtype=F32), GRID_W)
    c_idx = jnp.tile(jnp.arange(GRID_W, dtype=F32), rows)
    quarter = dim // 4
    omega = 1.0 / (10000.0 ** (jnp.arange(quarter, dtype=F32) / quarter))

    def emb(p):
        a = p[:, None] * omega[None, :]
        return jnp.concatenate([jnp.sin(a), jnp.cos(a)], -1)

    return jnp.concatenate([emb(r_idx), emb(c_idx)], -1)


def _centred_conv3(u, w, b):
    up = jnp.pad(u, ((0, 0), (1, 1), (0, 0)))
    return up[:, :-2] * w[0] + up[:, 1:-1] * w[1] + up[:, 2:] * w[2] + b


@functools.lru_cache(maxsize=None)
def _conv_dft_constants(n1, n_in):
    n = n1 * DFT_MINOR
    nk = n1 // 2 + 1
    k1 = np.arange(nk)
    ang = 2 * np.pi * np.outer(k1, np.arange(n_in)) / n1
    fwd = np.kron(np.concatenate([np.cos(ang), -np.sin(ang)], axis=0), np.eye(SUBLANES))
    phi = 2 * np.pi * np.outer(k1, np.arange(DFT_MINOR)) / n
    tw_c = jnp.asarray(np.repeat(np.cos(phi)[:, :, None], LANE_TILE, axis=2), F32)
    tw_s = jnp.asarray(np.repeat(np.sin(phi)[:, :, None], LANE_TILE, axis=2), F32)
    a2 = 2 * np.pi * np.outer(np.arange(DFT_MINOR), np.arange(DFT_MINOR)) / DFT_MINOR
    c2, s2 = np.cos(a2), np.sin(a2)
    minor_f = np.block([[c2, s2], [-s2, c2]])
    minor_i = np.block([[c2, -s2], [s2, c2]])
    w = np.where((k1 == 0) | (k1 == n1 // 2), 1.0, 2.0) / n
    ang_i = 2 * np.pi * np.outer(np.arange(n1 // 2), k1) / n1
    inv = np.kron(np.concatenate([np.cos(ang_i) * w, -np.sin(ang_i) * w], axis=1), np.eye(SUBLANES))
    return dict(nk=nk, fwd=_lhs_parts(fwd), tw_c=tw_c, tw_s=tw_s, minor_f=_lhs_parts(minor_f),
                minor_i=_lhs_parts(minor_i), inv=_lhs_parts(inv))


def _dft_major_fwd_kernel(g_ref, c_ref, s_ref, x_ref, o_ref):
    n_in, _, dc = x_ref.shape
    nk = c_ref.shape[0]
    xs = x_ref[...].reshape(n_in * SUBLANES, dc)
    a = jnp.dot(g_ref[...], _rhs_parts(xs), preferred_element_type=F32).reshape(2, nk, SUBLANES, dc)
    c = _lane_tiled(c_ref[...], dc)
    s = _lane_tiled(s_ref[...], dc)
    o_ref[0] = a[0] * c + a[1] * s
    o_ref[1] = a[1] * c - a[0] * s


def _dft_major_fwd(src, n_batch, batch_map, n_in, consts, width):
    bs, rows, dtot = src.shape
    nk = consts["nk"]
    view = src.reshape(bs, rows // DFT_MINOR, DFT_MINOR, dtot)
    dcb = width // DFT_DC
    g = consts["fwd"]

    def x_map(i, b, j):
        sb, lane_blk = batch_map(b)
        return (sb, 0, i, lane_blk * dcb + j)

    return pl.pallas_call(
        _dft_major_fwd_kernel,
        grid=(DFT_MINOR // SUBLANES, n_batch, dcb),
        in_specs=[pl.BlockSpec(g.shape, lambda i, b, j: (0, 0)),
                  pl.BlockSpec((nk, SUBLANES, LANE_TILE), lambda i, b, j: (0, i, 0)),
                  pl.BlockSpec((nk, SUBLANES, LANE_TILE), lambda i, b, j: (0, i, 0)),
                  pl.BlockSpec((None, n_in, SUBLANES, DFT_DC), x_map)],
        out_specs=pl.BlockSpec((None, 2, nk, SUBLANES, DFT_DC), lambda i, b, j: (b, 0, 0, i, j)),
        out_shape=jax.ShapeDtypeStruct((n_batch, 2, nk, DFT_MINOR, width), F32),
        compiler_params=pltpu.CompilerParams(dimension_semantics=("parallel", "parallel", "parallel")),
        name="dft_major_fwd",
    )(g, consts["tw_c"], consts["tw_s"], view)


def _dft_minor_fwd_kernel(mf_ref, a_ref, o_ref):
    dc = a_ref.shape[-1]
    a = a_ref[...].reshape(2 * DFT_MINOR, dc)
    o_ref[...] = jnp.dot(mf_ref[...], _rhs_parts(a), preferred_element_type=F32).reshape(2, DFT_MINOR, dc)


def _dft_minor_fwd(a, consts):
    n, _, nk, _, width = a.shape
    mf = consts["minor_f"]
    blk = pl.BlockSpec((None, 2, None, DFT_MINOR, DFT_DC), lambda k, j, b: (b, 0, k, 0, j))
    return pl.pallas_call(
        _dft_minor_fwd_kernel,
        grid=(nk, width // DFT_DC, n),
        in_specs=[pl.BlockSpec(mf.shape, lambda k, j, b: (0, 0)), blk],
        out_specs=blk,
        out_shape=jax.ShapeDtypeStruct(a.shape, F32),
        compiler_params=pltpu.CompilerParams(dimension_semantics=("parallel", "parallel", "parallel")),
        name="dft_minor_fwd",
    )(mf, a)


def _dft_minor_conv_kernel(mf_ref, mi_ref, kf_ref, a_ref, o_ref):
    dc = a_ref.shape[-1]
    a = a_ref[...].reshape(2 * DFT_MINOR, dc)
    x = jnp.dot(mf_ref[...], _rhs_parts(a), preferred_element_type=F32)
    xr, xi = x[:DFT_MINOR], x[DFT_MINOR:]
    kr, ki = kf_ref[0], kf_ref[1]
    y = jnp.concatenate([xr * kr - xi * ki, xr * ki + xi * kr], axis=0)
    o_ref[...] = jnp.dot(mi_ref[...], _rhs_parts(y), preferred_element_type=F32).reshape(2, DFT_MINOR, dc)


def _dft_minor_conv(a, kf, consts):
    n, _, nk, _, width = a.shape
    mf, mi = consts["minor_f"], consts["minor_i"]
    blk = pl.BlockSpec((None, 2, None, DFT_MINOR, DFT_DC), lambda k, j, b: (b, 0, k, 0, j))
    return pl.pallas_call(
        _dft_minor_conv_kernel,
        grid=(nk, width // DFT_DC, n),
        in_specs=[pl.BlockSpec(mf.shape, lambda k, j, b: (0, 0)),
                  pl.BlockSpec(mi.shape, lambda k, j, b: (0, 0)),
                  pl.BlockSpec((2, None, DFT_MINOR, DFT_DC), lambda k, j, b: (0, k, 0, j)),
                  blk],
        out_specs=blk,
        out_shape=jax.ShapeDtypeStruct(a.shape, F32),
        compiler_params=pltpu.CompilerParams(dimension_semantics=("parallel", "parallel", "parallel")),
        name="dft_minor_conv",
    )(mf, mi, kf, a)


def _dft_major_inv_gate_kernel(h_ref, c_ref, s_ref, bias_ref, b_ref, zin_ref, gate_ref, o_ref):
    _, nk, _, dc = b_ref.shape
    n_out = o_ref.shape[0]
    c = _lane_tiled(c_ref[...], dc)
    s = _lane_tiled(s_ref[...], dc)
    br, bi = b_ref[0], b_ref[1]
    bt = jnp.concatenate([br * c - bi * s, bi * c + br * s], axis=0).reshape(2 * nk * SUBLANES, dc)
    y = jnp.dot(h_ref[...], _rhs_parts(bt), preferred_element_type=F32).reshape(n_out, SUBLANES, dc)
    o_ref[...] = gate_ref[...] * (y + zin_ref[...] * bias_ref[...])


def _dft_major_inv_gate(bb, zsrc, zblk, gsrc, gblk, bias, consts):
    n, _, nk, _, width = bb.shape
    rows = zsrc.shape[1]
    n_out = rows // DFT_MINOR
    dcb = width // DFT_DC
    h = consts["inv"]
    zview = zsrc.reshape(n, n_out, DFT_MINOR, zsrc.shape[-1])
    gview = gsrc.reshape(n, n_out, DFT_MINOR, gsrc.shape[-1])
    tw = pl.BlockSpec((nk, SUBLANES, LANE_TILE), lambda i, b, j: (0, i, 0))
    out = pl.pallas_call(
        _dft_major_inv_gate_kernel,
        grid=(DFT_MINOR // SUBLANES, n, dcb),
        in_specs=[pl.BlockSpec(h.shape, lambda i, b, j: (0, 0)), tw, tw,
                  pl.BlockSpec((1, DFT_DC), lambda i, b, j: (0, j)),
                  pl.BlockSpec((None, 2, nk, SUBLANES, DFT_DC), lambda i, b, j: (b, 0, 0, i, j)),
                  pl.BlockSpec((None, n_out, SUBLANES, DFT_DC), lambda i, b, j: (b, 0, i, zblk * dcb + j)),
                  pl.BlockSpec((None, n_out, SUBLANES, DFT_DC), lambda i, b, j: (b, 0, i, gblk * dcb + j))],
        out_specs=pl.BlockSpec((None, n_out, SUBLANES, DFT_DC), lambda i, b, j: (b, 0, i, j)),
        out_shape=jax.ShapeDtypeStruct((n, n_out, DFT_MINOR, width), F32),
        compiler_params=pltpu.CompilerParams(dimension_semantics=("parallel", "parallel", "parallel")),
        name="dft_major_inv_gate",
    )(h, consts["tw_c"], consts["tw_s"], bias.reshape(1, width).astype(F32), bb, zview, gview)
    return out.reshape(n, rows, width)


def _hyena_two_sided(L, D, f_w1, f_b1, f_freq, f_w2, f_b2, f_w3):
    pos = jnp.arange(L, dtype=F32)
    t01 = pos / max(L - 1, 1)
    bands = (HY_EMB_DIM - 1) // 2
    f = jnp.linspace(1e-4, bands - 1, bands, dtype=F32)
    ang = f[None, :] * (2.0 * math.pi * pos / L)[:, None]
    z = jnp.concatenate([t01[:, None], jnp.cos(ang), -jnp.sin(ang)], -1)
    freq = f_freq.astype(F32)
    a = jnp.sin(freq * (z @ f_w1.astype(F32) + f_b1.astype(F32)))
    for n in range(HY_INNER_MLPS):
        a = jnp.sin(freq * (a @ f_w2[n].astype(F32) + f_b2[n].astype(F32)))
    h = (a @ f_w3.astype(F32)).reshape(L, 2, HY_ORDER, D)
    max_decay = math.log(HY_DECAY_TARGET) / HY_SHORT_DECAY_PCT
    min_decay = math.log(HY_DECAY_TARGET) / HY_LONG_DECAY_PCT
    deltas = jnp.linspace(min_decay, max_decay, D, dtype=F32)
    window = jnp.exp(-t01[:, None] * jnp.abs(deltas)[None, :])
    h = h * window[:, None, None, :]
    two_sided = jnp.concatenate([h[:, 0], jnp.zeros((1, HY_ORDER, D), F32), h[:0:-1, 1]], 0)
    return two_sided / jnp.sum(jnp.abs(two_sided), 0, keepdims=True)


def _hyena(h, w_in, b_in, conv_w, conv_b, f_w1, f_b1, f_freq, f_w2, f_b2, f_w3, f_bias, w_out, b_out):
    B_, L, D = h.shape
    n1 = 2 * L // DFT_MINOR
    u = _centred_conv3(h @ w_in + b_in, conv_w, conv_b)
    two_sided = _hyena_two_sided(L, D, f_w1, f_b1, f_freq, f_w2, f_b2, f_w3).reshape(1, 2 * L, HY_ORDER * D)
    kc = _conv_dft_constants(n1, n1)
    kf = _dft_minor_fwd(_dft_major_fwd(two_sided, HY_ORDER, lambda o: (0, o), n1, kc, D), kc)
    dc = _conv_dft_constants(n1, n1 // 2)

    def long_conv(zsrc, zblk, gblk, o):
        a = _dft_major_fwd(zsrc, B_, lambda b: (b, zblk), n1 // 2, dc, D)
        bb = _dft_minor_conv(a, kf[o], dc)
        return _dft_major_inv_gate(bb, zsrc, zblk, u, gblk, f_bias[o], dc)

    z = long_conv(u, 0, 1, 0)
    z = long_conv(z, 0, 2, 1)
    return z @ w_out + b_out


def _token_shift_bidir(h):
    hp = jnp.pad(h, ((0, 0), (1, 1), (0, 0)))
    return 0.5 * (hp[:, :-2] + hp[:, 2:]) - h


def _rwkv_prep(h, mu, wr, wk, wv, w0, w1, w2, a0, a1, a2, k_k, k_a):
    B_, T, D = h.shape
    H = D // RW_HEAD_DIM
    xx = _token_shift_bidir(h)
    xr, xw, xk, xv, xa, xg = [h + xx * mu[n] for n in range(6)]
    k = xk @ wk
    lw = jnp.einsum('nbtr,nrc->nbtc', jnp.tanh(jnp.einsum('btc,ncr->nbtr', xw, w1)), w2).astype(F32) + w0[:, None, None, :].astype(F32)
    log_decay = -math.exp(-0.5) * jax.nn.sigmoid(lw)
    a = jax.nn.sigmoid(jnp.einsum('nbtr,nrc->nbtc', jnp.einsum('btc,ncr->nbtr', xa, a1), a2).astype(F32) + a0[:, None, None, :].astype(F32))
    kk = (k * k_k).astype(F32).reshape(B_, T, H, RW_HEAD_DIM)
    kk = kk * lax.rsqrt(jnp.maximum(jnp.sum(kk * kk, -1, keepdims=True), 1e-24))
    kk = kk.reshape(B_, T, D)
    k_dir = k.astype(F32)[None] * (1.0 + (a - 1.0) * k_a.astype(F32))
    return {"r": (xr @ wr).astype(F32), "v": (xv @ wv).astype(F32), "kk": kk, "log_decay": log_decay,
            "k": k_dir, "kka": kk[None] * a, "xg": xg}


def _scan_kernel(r_ref, v_ref, kk_ref, lw_ref, k_ref, kka_ref, y_ref, s_ref):
    C = SCAN_CHUNK
    hd = RW_HEAD_DIM
    d = pl.program_id(0)

    @pl.when(pl.program_id(2) == 0)
    def _():
        s_ref[...] = jnp.zeros_like(s_ref)

    sign = 1 - 2 * d
    row = lax.broadcasted_iota(jnp.int32, (C, C), 0)
    col = lax.broadcasted_iota(jnp.int32, (C, C), 1)
    diff = (row - col) * sign
    incl = diff >= 0
    strict = diff > 0
    incl_b = incl.astype(BF16)
    eye = (row == col).astype(F32)
    heads_per_step = SCAN_LANES // hd

    def head_group(p, carry):
        sl = pl.ds(pl.multiple_of(p * SCAN_LANES, SCAN_LANES), SCAN_LANES)
        lw = lw_ref[0, 0, :, sl]
        cum = sum(jnp.dot(incl_b, part, preferred_element_type=F32) for part in _split3(lw))
        g_in = jnp.exp(cum)
        g_ex = jnp.exp(cum - lw)
        g_inv = jnp.exp(-cum)
        g_end = jnp.exp(jnp.sum(lw, axis=0, keepdims=True))
        a_t = -kk_ref[0, :, sl] * g_ex
        b_t = kka_ref[0, 0, :, sl] * g_inv
        k_t = k_ref[0, 0, :, sl] * g_inv
        r_t = r_ref[0, :, sl] * g_in
        v = v_ref[0, :, sl]
        hs = range(heads_per_step)
        cut = lambda t, hh: t[:, hh * hd:(hh + 1) * hd]
        a_h = [cut(a_t, hh) for hh in hs]
        r_h = [cut(r_t, hh) for hh in hs]
        v_h = [cut(v, hh) for hh in hs]
        bk = [jnp.concatenate([cut(b_t, hh), cut(k_t, hh)], axis=0) for hh in hs]
        sc = [_dot3(jnp.concatenate([a_h[hh], r_h[hh]], axis=0), bk[hh], _NT) for hh in hs]
        a_ab = [jnp.where(strict, s[:C, :C], 0.0) for s in sc]
        amk = [jnp.concatenate([jnp.where(strict, s[:C, C:], 0.0), jnp.where(incl, s[C:, C:], 0.0)], axis=0) for s in sc]
        m_rb = [jnp.where(incl, s[C:, :C], 0.0) for s in sc]
        av = [_dot3(amk[hh], v_h[hh]) for hh in hs]
        inv = [eye + a for a in a_ab]
        apow = [_dot3(a, a) for a in a_ab]
        n = 4
        while n < C:
            both = [_dot3(apow[hh], jnp.concatenate([inv[hh], apow[hh]], axis=1)) for hh in hs]
            inv = [inv[hh] + both[hh][:, :C] for hh in hs]
            apow = [both[hh][:, C:] for hh in hs]
            n *= 2
        inv = [inv[hh] + _dot3(apow[hh], inv[hh]) for hh in hs]
        wu = [_dot3(inv[hh], jnp.concatenate([a_h[hh], av[hh][:C]], axis=1)) for hh in hs]
        s0 = [s_ref[heads_per_step * p + hh] for hh in hs]
        ws = [_dot3(jnp.concatenate([wu[hh][:, :hd], r_h[hh]], axis=0), s0[hh], _NT) for hh in hs]
        u = [ws[hh][:C] + wu[hh][:, hd:] for hh in hs]
        s_add = [_dot3(jnp.concatenate([u[hh], v_h[hh]], axis=0), bk[hh], _TN) for hh in hs]
        for hh in hs:
            s_ref[heads_per_step * p + hh] = (s0[hh] + s_add[hh]) * cut(g_end, hh)
        ys = [ws[hh][C:] + _dot3(m_rb[hh], u[hh]) + av[hh][C:] for hh in hs]
        y_ref[0, 0, :, sl] = jnp.concatenate(ys, axis=1)
        return carry

    lax.fori_loop(0, lw_ref.shape[-1] // SCAN_LANES, head_group, 0)


def _wkv_scan(r, v, kk, log_decay, k_dir, kka, n_ctx):
    B_, T, D = r.shape
    C = SCAN_CHUNK
    assert n_ctx % C == 0 and T % C == 0 and D % SCAN_LANES == 0
    nc_ctx, nc = n_ctx // C, T // C

    def chunk(d, j):
        back = jnp.where(j < nc_ctx, nc_ctx - 1 - j, nc + nc_ctx - 1 - j)
        return jnp.where(d == 0, j, back)

    shared = pl.BlockSpec((1, C, D), lambda d, b, j: (b, chunk(d, j), 0))
    directed = pl.BlockSpec((1, 1, C, D), lambda d, b, j: (d, b, chunk(d, j), 0))
    return pl.pallas_call(
        _scan_kernel,
        grid=(2, B_, nc),
        in_specs=[shared, shared, shared, directed, directed, directed],
        out_specs=directed,
        out_shape=jax.ShapeDtypeStruct((2, B_, T, D), F32),
        scratch_shapes=[pltpu.VMEM((D // RW_HEAD_DIM, RW_HEAD_DIM, RW_HEAD_DIM), F32)],
        compiler_params=pltpu.CompilerParams(dimension_semantics=("parallel", "parallel", "arbitrary")),
        name="wkv_scan",
    )(r, v, kk, log_decay, k_dir, kka)


def _heads(t):
    return t.astype(F32).reshape(t.shape[:-1] + (t.shape[-1] // RW_HEAD_DIM, RW_HEAD_DIM))


def _rwkv_mixer(hc, hl, prep_params, g1, g2, r_k, gn_g, gn_b, w_o):
    pc = _rwkv_prep(hc, *prep_params)
    pl_ = _rwkv_prep(hl, *prep_params)
    n_ctx = hc.shape[1]

    def seq(name):
        return jnp.concatenate([pc[name], pl_[name]], axis=-2)

    y = _wkv_scan(seq("r"), seq("v"), seq("kk"), seq("log_decay"), seq("k"), seq("kka"), n_ctx)
    p, h = pl_, hl
    wkv = _heads(y[0, :, n_ctx:] + y[1, :, n_ctx:])
    m = jnp.mean(wkv, -1, keepdims=True)
    var = jnp.mean(jnp.square(wkv - m), -1, keepdims=True)
    gn = (wkv - m) * lax.rsqrt(var + RW_GN_EPS) * _heads(gn_g) + _heads(gn_b)
    bonus = jnp.sum(_heads(p["r"])[None] * _heads(p["k"]) * _heads(r_k), axis=(0, -1))[..., None] * _heads(p["v"])
    g = jax.nn.sigmoid(p["xg"] @ g1) @ g2
    o = (gn + bonus).reshape(h.shape).astype(h.dtype) * g
    return o @ w_o


@functools.lru_cache(maxsize=None)
def _fnet_dft_constants(n1, group):
    n = n1 * DFT_MINOR
    ang = 2 * np.pi * np.outer(np.arange(n1), np.arange(n1)) / n1
    c1, s1 = np.cos(ang), np.sin(ang)
    fwd = np.kron(np.block([[c1, s1], [-s1, c1]]), np.eye(SUBLANES))
    phi = 2 * np.pi * np.outer(np.arange(n1), np.arange(DFT_MINOR)) / n
    tw_c = jnp.asarray(np.repeat(np.cos(phi)[:, :, None], LANE_TILE, axis=2), F32)
    tw_s = jnp.asarray(np.repeat(np.sin(phi)[:, :, None], LANE_TILE, axis=2), F32)
    a2 = 2 * np.pi * np.outer(np.arange(DFT_MINOR), np.arange(DFT_MINOR)) / DFT_MINOR
    minor_re = np.concatenate([np.cos(a2), np.sin(a2)], axis=1) / math.sqrt(n * group)
    ag = 2 * np.pi * np.outer(np.arange(group), np.arange(group)) / group
    lane = np.concatenate([np.cos(ag), -np.sin(ag)], axis=1)
    lane_hi = lane.astype(ml_dtypes.bfloat16)
    lane_lo = (lane - lane_hi.astype(np.float64)).astype(ml_dtypes.bfloat16)
    lane_parts = jnp.asarray(np.concatenate([lane_hi, lane_hi, lane_lo], axis=0))
    return dict(nk=n1, fwd=_lhs_parts(fwd), tw_c=tw_c, tw_s=tw_s, minor_re=_lhs_parts(minor_re), lane=lane_parts)


def _lane_dft_kernel(f_ref, x_ref, o_ref):
    group = f_ref.shape[1] // 2
    for g in range(x_ref.shape[-1] // group):
        ls = slice(g * group, (g + 1) * group)
        z = jnp.dot(_rhs_parts_t(x_ref[:, ls]), f_ref[...], preferred_element_type=F32)
        o_ref[0, :, ls] = z[:, :group]
        o_ref[1, :, ls] = z[:, group:]


def _lane_dft(x, consts, group):
    B_, T, D = x.shape
    f = consts["lane"]
    return pl.pallas_call(
        _lane_dft_kernel,
        grid=(B_, T // FN_ROWS),
        in_specs=[pl.BlockSpec(f.shape, lambda b, t: (0, 0)),
                  pl.BlockSpec((None, FN_ROWS, D), lambda b, t: (b, t, 0))],
        out_specs=pl.BlockSpec((None, 2, FN_ROWS, D), lambda b, t: (b, 0, t, 0)),
        out_shape=jax.ShapeDtypeStruct((B_, 2, T, D), F32),
        compiler_params=pltpu.CompilerParams(dimension_semantics=("parallel", "parallel")),
        name="fnet_lane_dft",
    )(f, x)


def _dft_minor_real_kernel(m_ref, a_ref, o_ref):
    for q in range(SUBLANES):
        a = jnp.concatenate([a_ref[0, q], a_ref[1, q]], axis=0)
        o_ref[:, q, :] = jnp.dot(m_ref[...], _rhs_parts(a), preferred_element_type=F32)


def _dft_minor_real(a, consts):
    n, _, n1, _, width = a.shape
    m = consts["minor_re"]
    out = pl.pallas_call(
        _dft_minor_real_kernel,
        grid=(n1 // SUBLANES, width // DFT_DC, n),
        in_specs=[pl.BlockSpec(m.shape, lambda k, j, b: (0, 0)),
                  pl.BlockSpec((None, 2, SUBLANES, DFT_MINOR, DFT_DC), lambda k, j, b: (b, 0, k, 0, j))],
        out_specs=pl.BlockSpec((None, DFT_MINOR, SUBLANES, DFT_DC), lambda k, j, b: (b, 0, k, j)),
        out_shape=jax.ShapeDtypeStruct((n, DFT_MINOR, n1, width), F32),
        compiler_params=pltpu.CompilerParams(dimension_semantics=("parallel", "parallel", "parallel")),
        name="dft_minor_real",
    )(m, a)
    return out.reshape(n, DFT_MINOR * n1, width)


def _fourier(h, w_o, b_o):
    B_, T, D = h.shape
    n1 = T // DFT_MINOR
    group = D // FN_GROUPS
    consts = _fnet_dft_constants(n1, group)
    z = _lane_dft(h.astype(F32), consts, group).reshape(B_, 2 * T, D)
    a = _dft_major_fwd(z, B_, lambda b: (b, 0), 2 * n1, consts, D)
    mixed = _dft_minor_real(a, consts)
    return mixed.astype(h.dtype) @ w_o + b_o


def _expert_choice_moe(h, w_router, w1, w3, w2):
    B_, T, _ = h.shape
    cap = EC_CAPACITY * T // N_EXPERTS
    aff = jax.nn.softmax((h @ w_router).astype(F32), -1)
    gate, idx = lax.top_k(jnp.swapaxes(aff, 1, 2), cap)
    bidx = jnp.arange(B_)[:, None, None]
    xe = h[bidx, idx]
    he = jax.nn.silu(jnp.einsum('becd,edf->becf', xe, w1)) * jnp.einsum('becd,edf->becf', xe, w3)
    ye = jnp.einsum('becf,efd->becd', he, w2) * gate[..., None].astype(h.dtype)
    return jnp.zeros_like(h).at[bidx, idx].add(ye)


def kernel(x, c, ctx, c_ctx, mod_w, mod_b, ln_g, ln_b, moe_router, moe_w1, moe_w3, moe_w2, hy_w_in, hy_b_in, hy_conv_w, hy_conv_b, hy_f_w1, hy_f_b1, hy_f_freq, hy_f_w2, hy_f_b2, hy_f_w3, hy_f_bias, hy_w_out, hy_b_out, rw_mu, rw_wr, rw_wk, rw_wv, rw_w0, rw_w1, rw_w2, rw_a0, rw_a1, rw_a2, rw_kk, rw_ka, rw_g1, rw_g2, rw_rk, rw_gn_g, rw_gn_b, rw_wo, fn_wo, fn_bo):
    depth = mod_w.shape[0]
    D = x.shape[-1]
    alpha = (2 * depth) ** 0.25
    n_lat = x.shape[1]
    x = x + _grid_pos_embed(n_lat // GRID_W, D).astype(x.dtype)[None]
    xc = ctx
    readers = [i for i in range(depth) if i % N_MIXERS == 1]
    last_reader = readers[-1] if readers else -1
    for i in range(depth):
        kind, j = i % N_MIXERS, i // N_MIXERS
        ctx_in = i <= last_reader
        ctx_full = i < last_reader
        m_l = jnp.split((jax.nn.silu(c) @ mod_w[i] + mod_b[i])[:, None, :], 6, -1)
        hl = _modulate(x, m_l[0], m_l[1])
        if ctx_in:
            m_c = jnp.split((jax.nn.silu(c_ctx) @ mod_w[i] + mod_b[i])[None, None, :], 6, -1)
            hc = _modulate(xc, m_c[0], m_c[1])
        if kind == 0:
            hy = (hy_w_in[j], hy_b_in[j], hy_conv_w[j], hy_conv_b[j], hy_f_w1[j], hy_f_b1[j], hy_f_freq[j],
                  hy_f_w2[j], hy_f_b2[j], hy_f_w3[j], hy_f_bias[j], hy_w_out[j], hy_b_out[j])
            yl = _hyena(hl, *hy)
            yc = _hyena(hc, *hy) if ctx_full else None
        elif kind == 1:
            prep = (rw_mu[j], rw_wr[j], rw_wk[j], rw_wv[j], rw_w0[j], rw_w1[j], rw_w2[j],
                    rw_a0[j], rw_a1[j], rw_a2[j], rw_kk[j], rw_ka[j])
            assert not ctx_full
            yc = None
            yl = _rwkv_mixer(hc, hl, prep, rw_g1[j], rw_g2[j], rw_rk[j], rw_gn_g[j], rw_gn_b[j], rw_wo[j])
        else:
            yl = _fourier(hl, fn_wo[j], fn_bo[j])
            yc = _fourier(hc, fn_wo[j], fn_bo[j]) if ctx_full else None
        x = _ln_affine(alpha * x + (1.0 + m_l[2]) * yl, ln_g[i, 0], ln_b[i, 0])
        hl = _modulate(x, m_l[3], m_l[4])
        x = _ln_affine(alpha * x + (1.0 + m_l[5]) * _expert_choice_moe(hl, moe_router[i], moe_w1[i], moe_w3[i], moe_w2[i]),
                       ln_g[i, 1], ln_b[i, 1])
        if ctx_full:
            xc = _ln_affine(alpha * xc + (1.0 + m_c[2]) * yc, ln_g[i, 0], ln_b[i, 0])
            hc = _modulate(xc, m_c[3], m_c[4])
            xc = _ln_affine(alpha * xc + (1.0 + m_c[5]) * _expert_choice_moe(hc, moe_router[i], moe_w1[i], moe_w3[i], moe_w2[i]),
                            ln_g[i, 1], ln_b[i, 1])
    return x
```

```python
import functools
import math

import ml_dtypes
import numpy as np

import jax
import jax.numpy as jnp
from jax import lax
from jax.experimental import pallas as pl
from jax.experimental.pallas import tpu as pltpu

F32 = jnp.float32
BF16 = jnp.bfloat16

GRID_W = 64
N_MIXERS = 3
HY_ORDER = 2
HY_EMB_DIM = 33
HY_INNER_MLPS = 2
HY_DECAY_TARGET = 1e-2
HY_SHORT_DECAY_PCT = 0.3
HY_LONG_DECAY_PCT = 1.5
RW_HEAD_DIM = 64
RW_GN_EPS = 64e-5
FN_GROUPS = 8
N_EXPERTS = 16
EC_CAPACITY = 2
LN_EPS = 1e-5
ADALN_EPS = 1e-6

SUBLANES = 8
LANE_TILE = 128
SCAN_CHUNK = 64
SCAN_LANES = 512
DFT_MINOR = 128
DFT_DC = 1024
FN_ROWS = 512
MOE_ROWS = 1024
MOE_F_TILE = 512
MOE_SCATTER_UNROLL = 4
MOE_VMEM_BYTES = 56 * 1024 * 1024

_NT = (((1,), (1,)), ((), ()))
_TN = (((0,), (0,)), ((), ()))


def _split3(x):
    hi = x.astype(BF16)
    r1 = x - hi.astype(F32)
    mid = r1.astype(BF16)
    lo = (r1 - mid.astype(F32)).astype(BF16)
    return hi, mid, lo


def _split2(x):
    hi = x.astype(BF16)
    return hi, (x - hi.astype(F32)).astype(BF16)


def _dot3(a, b, dims=(((1,), (0,)), ((), ()))):
    a_hi, a_lo = _split2(a)
    b_hi, b_lo = _split2(b)
    dg = functools.partial(lax.dot_general, dimension_numbers=dims, preferred_element_type=F32)
    return dg(a_hi, b_hi) + (dg(a_hi, b_lo) + dg(a_lo, b_hi))


def _lhs_parts(m):
    hi = m.astype(ml_dtypes.bfloat16)
    lo = (m - hi.astype(np.float64)).astype(ml_dtypes.bfloat16)
    return jnp.asarray(np.concatenate([hi, hi, lo], axis=1))


def _rhs_parts(x):
    hi, lo = _split2(x)
    return jnp.concatenate([hi, lo, hi], axis=0)


def _rhs_parts_t(x):
    hi, lo = _split2(x)
    return jnp.concatenate([hi, lo, hi], axis=1)


def _lane_tiled(t, width):
    return jnp.concatenate([t] * (width // t.shape[-1]), axis=-1)


def _layer_norm(x, eps):
    xf = x.astype(F32)
    mu = jnp.mean(xf, -1, keepdims=True)
    var = jnp.mean(jnp.square(xf - mu), -1, keepdims=True)
    return (xf - mu) * lax.rsqrt(var + eps)


def _ln_affine(x, g, b):
    return (_layer_norm(x, LN_EPS) * g.astype(F32) + b.astype(F32)).astype(x.dtype)


def _modulate(x, shift, scale):
    return (_layer_norm(x, ADALN_EPS) * (1.0 + scale.astype(F32)) + shift.astype(F32)).astype(x.dtype)


def _grid_pos_embed(rows, dim):
    r_idx = jnp.repeat(jnp.arange(rows, dtype=F32), GRID_W)
    c_idx = jnp.tile(jnp.arange(GRID_W, dtype=F32), rows)
    quarter = dim // 4
    omega = 1.0 / (10000.0 ** (jnp.arange(quarter, dtype=F32) / quarter))

    def emb(p):
        a = p[:, None] * omega[None, :]
        return jnp.concatenate([jnp.sin(a), jnp.cos(a)], -1)

    return jnp.concatenate([emb(r_idx), emb(c_idx)], -1)


def _centred_conv3(u, w, b):
    up = jnp.pad(u, ((0, 0), (1, 1), (0, 0)))
    return up[:, :-2] * w[0] + up[:, 1:-1] * w[1] + up[:, 2:] * w[2] + b


@functools.lru_cache(maxsize=None)
def _conv_dft_constants(n1, n_in):
    n = n1 * DFT_MINOR
    nk = n1 // 2 + 1
    k1 = np.arange(nk)
    ang = 2 * np.pi * np.outer(k1, np.arange(n_in)) / n1
    fwd = np.kron(np.concatenate([np.cos(ang), -np.sin(ang)], axis=0), np.eye(SUBLANES))
    phi = 2 * np.pi * np.outer(k1, np.arange(DFT_MINOR)) / n
    tw_c = jnp.asarray(np.repeat(np.cos(phi)[:, :, None], LANE_TILE, axis=2), F32)
    tw_s = jnp.asarray(np.repeat(np.sin(phi)[:, :, None], LANE_TILE, axis=2), F32)
    a2 = 2 * np.pi * np.outer(np.arange(DFT_MINOR), np.arange(DFT_MINOR)) / DFT_MINOR
    c2, s2 = np.cos(a2), np.sin(a2)
    minor_f = np.block([[c2, s2], [-s2, c2]])
    minor_i = np.block([[c2, -s2], [s2, c2]])
    w = np.where((k1 == 0) | (k1 == n1 // 2), 1.0, 2.0) / n
    ang_i = 2 * np.pi * np.outer(np.arange(n1 // 2), k1) / n1
    inv = np.kron(np.concatenate([np.cos(ang_i) * w, -np.sin(ang_i) * w], axis=1), np.eye(SUBLANES))
    return dict(nk=nk, fwd=_lhs_parts(fwd), tw_c=tw_c, tw_s=tw_s, minor_f=_lhs_parts(minor_f),
                minor_i=_lhs_parts(minor_i), inv=_lhs_parts(inv))


def _dft_major_fwd_kernel(g_ref, c_ref, s_ref, x_ref, o_ref):
    n_in, _, dc = x_ref.shape
    nk = c_ref.shape[0]
    xs = x_ref[...].reshape(n_in * SUBLANES, dc)
    a = jnp.dot(g_ref[...], _rhs_parts(xs), preferred_element_type=F32).reshape(2, nk, SUBLANES, dc)
    c = _lane_tiled(c_ref[...], dc)
    s = _lane_tiled(s_ref[...], dc)
    o_ref[0] = a[0] * c + a[1] * s
    o_ref[1] = a[1] * c - a[0] * s


def _dft_major_fwd(src, n_batch, batch_map, n_in, consts, width):
    bs, rows, dtot = src.shape
    nk = consts["nk"]
    view = src.reshape(bs, rows // DFT_MINOR, DFT_MINOR, dtot)
    dcb = width // DFT_DC
    g = consts["fwd"]

    def x_map(i, b, j):
        sb, lane_blk = batch_map(b)
        return (sb, 0, i, lane_blk * dcb + j)

    return pl.pallas_call(
        _dft_major_fwd_kernel,
        grid=(DFT_MINOR // SUBLANES, n_batch, dcb),
        in_specs=[pl.BlockSpec(g.shape, lambda i, b, j: (0, 0)),
                  pl.BlockSpec((nk, SUBLANES, LANE_TILE), lambda i, b, j: (0, i, 0)),
                  pl.BlockSpec((nk, SUBLANES, LANE_TILE), lambda i, b, j: (0, i, 0)),
                  pl.BlockSpec((None, n_in, SUBLANES, DFT_DC), x_map)],
        out_specs=pl.BlockSpec((None, 2, nk, SUBLANES, DFT_DC), lambda i, b, j: (b, 0, 0, i, j)),
        out_shape=jax.ShapeDtypeStruct((n_batch, 2, nk, DFT_MINOR, width), F32),
        compiler_params=pltpu.CompilerParams(dimension_semantics=("parallel", "parallel", "parallel")),
        name="dft_major_fwd",
    )(g, consts["tw_c"], consts["tw_s"], view)


def _dft_minor_fwd_kernel(mf_ref, a_ref, o_ref):
    dc = a_ref.shape[-1]
    a = a_ref[...].reshape(2 * DFT_MINOR, dc)
    o_ref[...] = jnp.dot(mf_ref[...], _rhs_parts(a), preferred_element_type=F32).reshape(2, DFT_MINOR, dc)


def _dft_minor_fwd(a, consts):
    n, _, nk, _, width = a.shape
    mf = consts["minor_f"]
    blk = pl.BlockSpec((None, 2, None, DFT_MINOR, DFT_DC), lambda k, j, b: (b, 0, k, 0, j))
    return pl.pallas_call(
        _dft_minor_fwd_kernel,
        grid=(nk, width // DFT_DC, n),
        in_specs=[pl.BlockSpec(mf.shape, lambda k, j, b: (0, 0)), blk],
        out_specs=blk,
        out_shape=jax.ShapeDtypeStruct(a.shape, F32),
        compiler_params=pltpu.CompilerParams(dimension_semantics=("parallel", "parallel", "parallel")),
        name="dft_minor_fwd",
    )(mf, a)


def _dft_minor_conv_kernel(mf_ref, mi_ref, kf_ref, a_ref, o_ref):
    dc = a_ref.shape[-1]
    a = a_ref[...].reshape(2 * DFT_MINOR, dc)
    x = jnp.dot(mf_ref[...], _rhs_parts(a), preferred_element_type=F32)
    xr, xi = x[:DFT_MINOR], x[DFT_MINOR:]
    kr, ki = kf_ref[0], kf_ref[1]
    y = jnp.concatenate([xr * kr - xi * ki, xr * ki + xi * kr], axis=0)
    o_ref[...] = jnp.dot(mi_ref[...], _rhs_parts(y), preferred_element_type=F32).reshape(2, DFT_MINOR, dc)


def _dft_minor_conv(a, kf, consts):
    n, _, nk, _, width = a.shape
    mf, mi = consts["minor_f"], consts["minor_i"]
    blk = pl.BlockSpec((None, 2, None, DFT_MINOR, DFT_DC), lambda k, j, b: (b, 0, k, 0, j))
    return pl.pallas_call(
        _dft_minor_conv_kernel,
        grid=(nk, width // DFT_DC, n),
        in_specs=[pl.BlockSpec(mf.shape, lambda k, j, b: (0, 0)),
                  pl.BlockSpec(mi.shape, lambda k, j, b: (0, 0)),
                  pl.BlockSpec((2, None, DFT_MINOR, DFT_DC), lambda k, j, b: (0, k, 0, j)),
                  blk],
        out_specs=blk,
        out_shape=jax.ShapeDtypeStruct(a.shape, F32),
        compiler_params=pltpu.CompilerParams(dimension_semantics=("parallel", "parallel", "parallel")),
        name="dft_minor_conv",
    )(mf, mi, kf, a)


def _dft_major_inv_gate_kernel(h_ref, c_ref, s_ref, bias_ref, b_ref, zin_ref, gate_ref, o_ref):
    _, nk, _, dc = b_ref.shape
    n_out = o_ref.shape[0]
    c = _lane_tiled(c_ref[...], dc)
    s = _lane_tiled(s_ref[...], dc)
    br, bi = b_ref[0], b_ref[1]
    bt = jnp.concatenate([br * c - bi * s, bi * c + br * s], axis=0).reshape(2 * nk * SUBLANES, dc)
    y = jnp.dot(h_ref[...], _rhs_parts(bt), preferred_element_type=F32).reshape(n_out, SUBLANES, dc)
    o_ref[...] = gate_ref[...] * (y + zin_ref[...] * bias_ref[...])


def _dft_major_inv_gate(bb, zsrc, zblk, gsrc, gblk, bias, consts):
    n, _, nk, _, width = bb.shape
    rows = zsrc.shape[1]
    n_out = rows // DFT_MINOR
    dcb = width // DFT_DC
    h = consts["inv"]
    zview = zsrc.reshape(n, n_out, DFT_MINOR, zsrc.shape[-1])
    gview = gsrc.reshape(n, n_out, DFT_MINOR, gsrc.shape[-1])
    tw = pl.BlockSpec((nk, SUBLANES, LANE_TILE), lambda i, b, j: (0, i, 0))
    out = pl.pallas_call(
        _dft_major_inv_gate_kernel,
        grid=(DFT_MINOR // SUBLANES, n, dcb),
        in_specs=[pl.BlockSpec(h.shape, lambda i, b, j: (0, 0)), tw, tw,
                  pl.BlockSpec((1, DFT_DC), lambda i, b, j: (0, j)),
                  pl.BlockSpec((None, 2, nk, SUBLANES, DFT_DC), lambda i, b, j: (b, 0, 0, i, j)),
                  pl.BlockSpec((None, n_out, SUBLANES, DFT_DC), lambda i, b, j: (b, 0, i, zblk * dcb + j)),
                  pl.BlockSpec((None, n_out, SUBLANES, DFT_DC), lambda i, b, j: (b, 0, i, gblk * dcb + j))],
        out_specs=pl.BlockSpec((None, n_out, SUBLANES, DFT_DC), lambda i, b, j: (b, 0, i, j)),
        out_shape=jax.ShapeDtypeStruct((n, n_out, DFT_MINOR, width), F32),
        compiler_params=pltpu.CompilerParams(dimension_semantics=("parallel", "parallel", "parallel")),
        name="dft_major_inv_gate",
    )(h, consts["tw_c"], consts["tw_s"], bias.reshape(1, width).astype(F32), bb, zview, gview)
    return out.reshape(n, rows, width)


def _hyena_two_sided(L, D, f_w1, f_b1, f_freq, f_w2, f_b2, f_w3):
    pos = jnp.arange(L, dtype=F32)
    t01 = pos / max(L - 1, 1)
    bands = (HY_EMB_DIM - 1) // 2
    f = jnp.linspace(1e-4, bands - 1, bands, dtype=F32)
    ang = f[None, :] * (2.0 * math.pi * pos / L)[:, None]
    z = jnp.concatenate([t01[:, None], jnp.cos(ang), -jnp.sin(ang)], -1)
    freq = f_freq.astype(F32)
    a = jnp.sin(freq * (z @ f_w1.astype(F32) + f_b1.astype(F32)))
    for n in range(HY_INNER_MLPS):
        a = jnp.sin(freq * (a @ f_w2[n].astype(F32) + f_b2[n].astype(F32)))
    h = (a @ f_w3.astype(F32)).reshape(L, 2, HY_ORDER, D)
    max_decay = math.log(HY_DECAY_TARGET) / HY_SHORT_DECAY_PCT
    min_decay = math.log(HY_DECAY_TARGET) / HY_LONG_DECAY_PCT
    deltas = jnp.linspace(min_decay, max_decay, D, dtype=F32)
    window = jnp.exp(-t01[:, None] * jnp.abs(deltas)[None, :])
    h = h * window[:, None, None, :]
    two_sided = jnp.concatenate([h[:, 0], jnp.zeros((1, HY_ORDER, D), F32), h[:0:-1, 1]], 0)
    return two_sided / jnp.sum(jnp.abs(two_sided), 0, keepdims=True)


def _hyena(h, w_in, b_in, conv_w, conv_b, f_w1, f_b1, f_freq, f_w2, f_b2, f_w3, f_bias, w_out, b_out):
    B_, L, D = h.shape
    n1 = 2 * L // DFT_MINOR
    u = _centred_conv3(h @ w_in + b_in, conv_w, conv_b)
    two_sided = _hyena_two_sided(L, D, f_w1, f_b1, f_freq, f_w2, f_b2, f_w3).reshape(1, 2 * L, HY_ORDER * D)
    kc = _conv_dft_constants(n1, n1)
    kf = _dft_minor_fwd(_dft_major_fwd(two_sided, HY_ORDER, lambda o: (0, o), n1, kc, D), kc)
    dc = _conv_dft_constants(n1, n1 // 2)

    def long_conv(zsrc, zblk, gblk, o):
        a = _dft_major_fwd(zsrc, B_, lambda b: (b, zblk), n1 // 2, dc, D)
        bb = _dft_minor_conv(a, kf[o], dc)
        return _dft_major_inv_gate(bb, zsrc, zblk, u, gblk, f_bias[o], dc)

    z = long_conv(u, 0, 1, 0)
    z = long_conv(z, 0, 2, 1)
    return z @ w_out + b_out


def _token_shift_bidir(h):
    hp = jnp.pad(h, ((0, 0), (1, 1), (0, 0)))
    return 0.5 * (hp[:, :-2] + hp[:, 2:]) - h


def _rwkv_prep(h, mu, wr, wk, wv, w0, w1, w2, a0, a1, a2, k_k, k_a):
    B_, T, D = h.shape
    H = D // RW_HEAD_DIM
    xx = _token_shift_bidir(h)
    xr, xw, xk, xv, xa, xg = [h + xx * mu[n] for n in range(6)]
    k = xk @ wk
    lw = jnp.einsum('nbtr,nrc->nbtc', jnp.tanh(jnp.einsum('btc,ncr->nbtr', xw, w1)), w2).astype(F32) + w0[:, None, None, :].astype(F32)
    log_decay = -math.exp(-0.5) * jax.nn.sigmoid(lw)
    a = jax.nn.sigmoid(jnp.einsum('nbtr,nrc->nbtc', jnp.einsum('btc,ncr->nbtr', xa, a1), a2).astype(F32) + a0[:, None, None, :].astype(F32))
    kk = (k * k_k).astype(F32).reshape(B_, T, H, RW_HEAD_DIM)
    kk = kk * lax.rsqrt(jnp.maximum(jnp.sum(kk * kk, -1, keepdims=True), 1e-24))
    kk = kk.reshape(B_, T, D)
    k_dir = k.astype(F32)[None] * (1.0 + (a - 1.0) * k_a.astype(F32))
    return {"r": (xr @ wr).astype(F32), "v": (xv @ wv).astype(F32), "kk": kk, "log_decay": log_decay,
            "k": k_dir, "kka": kk[None] * a, "xg": xg}


def _scan_kernel(r_ref, v_ref, kk_ref, lw_ref, k_ref, kka_ref, y_ref, s_ref):
    C = SCAN_CHUNK
    hd = RW_HEAD_DIM
    d = pl.program_id(0)

    @pl.when(pl.program_id(2) == 0)
    def _():
        s_ref[...] = jnp.zeros_like(s_ref)

    sign = 1 - 2 * d
    row = lax.broadcasted_iota(jnp.int32, (C, C), 0)
    col = lax.broadcasted_iota(jnp.int32, (C, C), 1)
    diff = (row - col) * sign
    incl = diff >= 0
    strict = diff > 0
    incl_b = incl.astype(BF16)
    eye = (row == col).astype(F32)
    heads_per_step = SCAN_LANES // hd

    def head_group(p, carry):
        sl = pl.ds(pl.multiple_of(p * SCAN_LANES, SCAN_LANES), SCAN_LANES)
        lw = lw_ref[0, 0, :, sl]
        cum = sum(jnp.dot(incl_b, part, preferred_element_type=F32) for part in _split3(lw))
        g_in = jnp.exp(cum)
        g_ex = jnp.exp(cum - lw)
        g_inv = jnp.exp(-cum)
        g_end = jnp.exp(jnp.sum(lw, axis=0, keepdims=True))
        a_t = -kk_ref[0, :, sl] * g_ex
        b_t = kka_ref[0, 0, :, sl] * g_inv
        k_t = k_ref[0, 0, :, sl] * g_inv
        r_t = r_ref[0, :, sl] * g_in
        v = v_ref[0, :, sl]
        hs = range(heads_per_step)
        cut = lambda t, hh: t[:, hh * hd:(hh + 1) * hd]
        a_h = [cut(a_t, hh) for hh in hs]
        r_h = [cut(r_t, hh) for hh in hs]
        v_h = [cut(v, hh) for hh in hs]
        bk = [jnp.concatenate([cut(b_t, hh), cut(k_t, hh)], axis=0) for hh in hs]
        sc = [_dot3(jnp.concatenate([a_h[hh], r_h[hh]], axis=0), bk[hh], _NT) for hh in hs]
        a_ab = [jnp.where(strict, s[:C, :C], 0.0) for s in sc]
        amk = [jnp.concatenate([jnp.where(strict, s[:C, C:], 0.0), jnp.where(incl, s[C:, C:], 0.0)], axis=0) for s in sc]
        m_rb = [jnp.where(incl, s[C:, :C], 0.0) for s in sc]
        av = [_dot3(amk[hh], v_h[hh]) for hh in hs]
        inv = [eye + a for a in a_ab]
        apow = [_dot3(a, a) for a in a_ab]
        n = 4
        while n < C:
            both = [_dot3(apow[hh], jnp.concatenate([inv[hh], apow[hh]], axis=1)) for hh in hs]
            inv = [inv[hh] + both[hh][:, :C] for hh in hs]
            apow = [both[hh][:, C:] for hh in hs]
            n *= 2
        inv = [inv[hh] + _dot3(apow[hh], inv[hh]) for hh in hs]
        wu = [_dot3(inv[hh], jnp.concatenate([a_h[hh], av[hh][:C]], axis=1)) for hh in hs]
        s0 = [s_ref[heads_per_step * p + hh] for hh in hs]
        ws = [_dot3(jnp.concatenate([wu[hh][:, :hd], r_h[hh]], axis=0), s0[hh], _NT) for hh in hs]
        u = [ws[hh][:C] + wu[hh][:, hd:] for hh in hs]
        s_add = [_dot3(jnp.concatenate([u[hh], v_h[hh]], axis=0), bk[hh], _TN) for hh in hs]
        for hh in hs:
            s_ref[heads_per_step * p + hh] = (s0[hh] + s_add[hh]) * cut(g_end, hh)
        ys = [ws[hh][C:] + _dot3(m_rb[hh], u[hh]) + av[hh][C:] for hh in hs]
        y_ref[0, 0, :, sl] = jnp.concatenate(ys, axis=1)
        return carry

    lax.fori_loop(0, lw_ref.shape[-1] // SCAN_LANES, head_group, 0)


def _wkv_scan(r, v, kk, log_decay, k_dir, kka, n_ctx):
    B_, T, D = r.shape
    C = SCAN_CHUNK
    assert n_ctx % C == 0 and T % C == 0 and D % SCAN_LANES == 0
    nc_ctx, nc = n_ctx // C, T // C

    def chunk(d, j):
        back = jnp.where(j < nc_ctx, nc_ctx - 1 - j, nc + nc_ctx - 1 - j)
        return jnp.where(d == 0, j, back)

    shared = pl.BlockSpec((1, C, D), lambda d, b, j: (b, chunk(d, j), 0))
    directed = pl.BlockSpec((1, 1, C, D), lambda d, b, j: (d, b, chunk(d, j), 0))
    return pl.pallas_call(
        _scan_kernel,
        grid=(2, B_, nc),
        in_specs=[shared, shared, shared, directed, directed, directed],
        out_specs=directed,
        out_shape=jax.ShapeDtypeStruct((2, B_, T, D), F32),
        scratch_shapes=[pltpu.VMEM((D // RW_HEAD_DIM, RW_HEAD_DIM, RW_HEAD_DIM), F32)],
        compiler_params=pltpu.CompilerParams(dimension_semantics=("parallel", "parallel", "arbitrary")),
        name="wkv_scan",
    )(r, v, kk, log_decay, k_dir, kka)


def _heads(t):
    return t.astype(F32).reshape(t.shape[:-1] + (t.shape[-1] // RW_HEAD_DIM, RW_HEAD_DIM))


def _rwkv_mixer(hc, hl, prep_params, g1, g2, r_k, gn_g, gn_b, w_o):
    pc = _rwkv_prep(hc, *prep_params)
    pl_ = _rwkv_prep(hl, *prep_params)
    n_ctx = hc.shape[1]

    def seq(name):
        return jnp.concatenate([pc[name], pl_[name]], axis=-2)

    y = _wkv_scan(seq("r"), seq("v"), seq("kk"), seq("log_decay"), seq("k"), seq("kka"), n_ctx)
    p, h = pl_, hl
    wkv = _heads(y[0, :, n_ctx:] + y[1, :, n_ctx:])
    m = jnp.mean(wkv, -1, keepdims=True)
    var = jnp.mean(jnp.square(wkv - m), -1, keepdims=True)
    gn = (wkv - m) * lax.rsqrt(var + RW_GN_EPS) * _heads(gn_g) + _heads(gn_b)
    bonus = jnp.sum(_heads(p["r"])[None] * _heads(p["k"]) * _heads(r_k), axis=(0, -1))[..., None] * _heads(p["v"])
    g = jax.nn.sigmoid(p["xg"] @ g1) @ g2
    o = (gn + bonus).reshape(h.shape).astype(h.dtype) * g
    return o @ w_o


@functools.lru_cache(maxsize=None)
def _fnet_dft_constants(n1, group):
    n = n1 * DFT_MINOR
    ang = 2 * np.pi * np.outer(np.arange(n1), np.arange(n1)) / n1
    c1, s1 = np.cos(ang), np.sin(ang)
    fwd = np.kron(np.block([[c1, s1], [-s1, c1]]), np.eye(SUBLANES))
    phi = 2 * np.pi * np.outer(np.arange(n1), np.arange(DFT_MINOR)) / n
    tw_c = jnp.asarray(np.repeat(np.cos(phi)[:, :, None], LANE_TILE, axis=2), F32)
    tw_s = jnp.asarray(np.repeat(np.sin(phi)[:, :, None], LANE_TILE, axis=2), F32)
    a2 = 2 * np.pi * np.outer(np.arange(DFT_MINOR), np.arange(DFT_MINOR)) / DFT_MINOR
    minor_re = np.concatenate([np.cos(a2), np.sin(a2)], axis=1) / math.sqrt(n * group)
    ag = 2 * np.pi * np.outer(np.arange(group), np.arange(group)) / group
    lane = np.concatenate([np.cos(ag), -np.sin(ag)], axis=1)
    lane_hi = lane.astype(ml_dtypes.bfloat16)
    lane_lo = (lane - lane_hi.astype(np.float64)).astype(ml_dtypes.bfloat16)
    lane_parts = jnp.asarray(np.concatenate([lane_hi, lane_hi, lane_lo], axis=0))
    return dict(nk=n1, fwd=_lhs_parts(fwd), tw_c=tw_c, tw_s=tw_s, minor_re=_lhs_parts(minor_re), lane=lane_parts)


def _lane_dft_kernel(f_ref, x_ref, o_ref):
    group = f_ref.shape[1] // 2
    for g in range(x_ref.shape[-1] // group):
        ls = slice(g * group, (g + 1) * group)
        z = jnp.dot(_rhs_parts_t(x_ref[:, ls]), f_ref[...], preferred_element_type=F32)
        o_ref[0, :, ls] = z[:, :group]
        o_ref[1, :, ls] = z[:, group:]


def _lane_dft(x, consts, group):
    B_, T, D = x.shape
    f = consts["lane"]
    return pl.pallas_call(
        _lane_dft_kernel,
        grid=(B_, T // FN_ROWS),
        in_specs=[pl.BlockSpec(f.shape, lambda b, t: (0, 0)),
                  pl.BlockSpec((None, FN_ROWS, D), lambda b, t: (b, t, 0))],
        out_specs=pl.BlockSpec((None, 2, FN_ROWS, D), lambda b, t: (b, 0, t, 0)),
        out_shape=jax.ShapeDtypeStruct((B_, 2, T, D), F32),
        compiler_params=pltpu.CompilerParams(dimension_semantics=("parallel", "parallel")),
        name="fnet_lane_dft",
    )(f, x)


def _dft_minor_real_kernel(m_ref, a_ref, o_ref):
    for q in range(SUBLANES):
        a = jnp.concatenate([a_ref[0, q], a_ref[1, q]], axis=0)
        o_ref[:, q, :] = jnp.dot(m_ref[...], _rhs_parts(a), preferred_element_type=F32)


def _dft_minor_real(a, consts):
    n, _, n1, _, width = a.shape
    m = consts["minor_re"]
    out = pl.pallas_call(
        _dft_minor_real_kernel,
        grid=(n1 // SUBLANES, width // DFT_DC, n),
        in_specs=[pl.BlockSpec(m.shape, lambda k, j, b: (0, 0)),
                  pl.BlockSpec((None, 2, SUBLANES, DFT_MINOR, DFT_DC), lambda k, j, b: (b, 0, k, 0, j))],
        out_specs=pl.BlockSpec((None, DFT_MINOR, SUBLANES, DFT_DC), lambda k, j, b: (b, 0, k, j)),
        out_shape=jax.ShapeDtypeStruct((n, DFT_MINOR, n1, width), F32),
        compiler_params=pltpu.CompilerParams(dimension_semantics=("parallel", "parallel", "parallel")),
        name="dft_minor_real",
    )(m, a)
    return out.reshape(n, DFT_MINOR * n1, width)


def _fourier(h, w_o, b_o):
    B_, T, D = h.shape
    n1 = T // DFT_MINOR
    group = D // FN_GROUPS
    consts = _fnet_dft_constants(n1, group)
    z = _lane_dft(h.astype(F32), consts, group).reshape(B_, 2 * T, D)
    a = _dft_major_fwd(z, B_, lambda b: (b, 0), 2 * n1, consts, D)
    mixed = _dft_minor_real(a, consts)
    return mixed.astype(h.dtype) @ w_o + b_o


def _moe_ffn_kernel(x_ref, g_ref, w1_ref, w3_ref, w2_ref, o_ref, xb_ref):
    f = pl.program_id(2)

    @pl.when(f == 0)
    def _():
        xb_ref[...] = x_ref[...].reshape(xb_ref.shape).astype(BF16)

    xb = xb_ref[...]
    h1 = jnp.dot(xb, w1_ref[...].astype(BF16), preferred_element_type=F32)
    h3 = jnp.dot(xb, w3_ref[...].astype(BF16), preferred_element_type=F32)
    he = (h1 * jax.nn.sigmoid(h1) * h3).astype(BF16)
    part = jnp.dot(he, w2_ref[...].astype(BF16), preferred_element_type=F32).reshape(o_ref.shape)

    @pl.when(f == 0)
    def _():
        o_ref[...] = part

    @pl.when(f > 0)
    def _():
        o_ref[...] += part

    @pl.when(f == pl.num_programs(2) - 1)
    def _():
        o_ref[...] *= g_ref[...]


def _moe_ffn(xe, gate, w1, w3, w2):
    B_, E, cap, D = xe.shape
    F = w1.shape[-1]
    mb = max(1, min(B_, MOE_ROWS // cap))
    ft = min(F, MOE_F_TILE)
    assert B_ % mb == 0 and F % ft == 0
    tok = pl.BlockSpec((mb, None, cap, D), lambda e, m, f: (m, e, 0, 0))
    return pl.pallas_call(
        _moe_ffn_kernel,
        grid=(E, B_ // mb, F // ft),
        in_specs=[tok,
                  pl.BlockSpec((mb, None, cap, 1), lambda e, m, f: (m, e, 0, 0)),
                  pl.BlockSpec((None, D, ft), lambda e, m, f: (e, 0, f)),
                  pl.BlockSpec((None, D, ft), lambda e, m, f: (e, 0, f)),
                  pl.BlockSpec((None, ft, D), lambda e, m, f: (e, f, 0))],
        out_specs=tok,
        out_shape=jax.ShapeDtypeStruct((B_, E, cap, D), F32),
        scratch_shapes=[pltpu.VMEM((mb * cap, D), BF16)],
        compiler_params=pltpu.CompilerParams(dimension_semantics=("parallel", "parallel", "arbitrary"),
                                             vmem_limit_bytes=MOE_VMEM_BYTES),
        name="moe_ffn",
    )(xe, gate[..., None].astype(F32), w1, w3, w2)


def _moe_scatter_kernel(idx_ref, ye_ref, buf_ref, o_ref):
    del buf_ref
    e = pl.program_id(0)
    cap = ye_ref.shape[0]

    @pl.when(e == 0)
    def _():
        o_ref[...] = jnp.zeros_like(o_ref)

    def group(g, carry):
        rows = []
        for u in range(MOE_SCATTER_UNROLL):
            j = g * MOE_SCATTER_UNROLL + u
            t = idx_ref[e * cap + j]
            rows.append((t, o_ref[pl.ds(t, 1), :] + ye_ref[pl.ds(j, 1), :]))
        for t, r in rows:
            o_ref[pl.ds(t, 1), :] = r
        return carry

    lax.fori_loop(0, cap // MOE_SCATTER_UNROLL, group, 0)


def _moe_scatter(ye, idx, T):
    B_, E, cap, D = ye.shape
    assert cap % MOE_SCATTER_UNROLL == 0
    out = jnp.zeros((B_, T, D), F32)
    for b in range(B_):
        out = pl.pallas_call(
            _moe_scatter_kernel,
            grid_spec=pltpu.PrefetchScalarGridSpec(
                num_scalar_prefetch=1,
                grid=(E,),
                in_specs=[pl.BlockSpec((None, None, cap, D), lambda e, idx_ref, b=b: (b, e, 0, 0)),
                          pl.BlockSpec(memory_space=pl.ANY)],
                out_specs=pl.BlockSpec((None, T, D), lambda e, idx_ref, b=b: (b, 0, 0)),
            ),
            out_shape=jax.ShapeDtypeStruct((B_, T, D), F32),
            input_output_aliases={2: 0},
            compiler_params=pltpu.CompilerParams(dimension_semantics=("arbitrary",),
                                                 vmem_limit_bytes=MOE_VMEM_BYTES),
            name="moe_scatter",
        )(idx[b].reshape(E * cap).astype(jnp.int32), ye, out)
    return out


def _expert_choice_moe(h, w_router, w1, w3, w2):
    B_, T, _ = h.shape
    cap = EC_CAPACITY * T // N_EXPERTS
    aff = jax.nn.softmax((h @ w_router).astype(F32), -1)
    gate, idx = lax.top_k(jnp.swapaxes(aff, 1, 2), cap)
    bidx = jnp.arange(B_)[:, None, None]
    xe = h[bidx, idx]
    ye = _moe_ffn(xe, gate, w1, w3, w2)
    return _moe_scatter(ye, idx, T).astype(h.dtype)


def kernel(x, c, ctx, c_ctx, mod_w, mod_b, ln_g, ln_b, moe_router, moe_w1, moe_w3, moe_w2, hy_w_in, hy_b_in, hy_conv_w, hy_conv_b, hy_f_w1, hy_f_b1, hy_f_freq, hy_f_w2, hy_f_b2, hy_f_w3, hy_f_bias, hy_w_out, hy_b_out, rw_mu, rw_wr, rw_wk, rw_wv, rw_w0, rw_w1, rw_w2, rw_a0, rw_a1, rw_a2, rw_kk, rw_ka, rw_g1, rw_g2, rw_rk, rw_gn_g, rw_gn_b, rw_wo, fn_wo, fn_bo):
    depth = mod_w.shape[0]
    D = x.shape[-1]
    alpha = (2 * depth) ** 0.25
    n_lat = x.shape[1]
    x = x + _grid_pos_embed(n_lat // GRID_W, D).astype(x.dtype)[None]
    xc = ctx
    readers = [i for i in range(depth) if i % N_MIXERS == 1]
    last_reader = readers[-1] if readers else -1
    for i in range(depth):
        kind, j = i % N_MIXERS, i // N_MIXERS
        ctx_in = i <= last_reader
        ctx_full = i < last_reader
        m_l = jnp.split((jax.nn.silu(c) @ mod_w[i] + mod_b[i])[:, None, :], 6, -1)
        hl = _modulate(x, m_l[0], m_l[1])
        if ctx_in:
            m_c = jnp.split((jax.nn.silu(c_ctx) @ mod_w[i] + mod_b[i])[None, None, :], 6, -1)
            hc = _modulate(xc, m_c[0], m_c[1])
        if kind == 0:
            hy = (hy_w_in[j], hy_b_in[j], hy_conv_w[j], hy_conv_b[j], hy_f_w1[j], hy_f_b1[j], hy_f_freq[j],
                  hy_f_w2[j], hy_f_b2[j], hy_f_w3[j], hy_f_bias[j], hy_w_out[j], hy_b_out[j])
            yl = _hyena(hl, *hy)
            yc = _hyena(hc, *hy) if ctx_full else None
        elif kind == 1:
            prep = (rw_mu[j], rw_wr[j], rw_wk[j], rw_wv[j], rw_w0[j], rw_w1[j], rw_w2[j],
                    rw_a0[j], rw_a1[j], rw_a2[j], rw_kk[j], rw_ka[j])
            assert not ctx_full
            yc = None
            yl = _rwkv_mixer(hc, hl, prep, rw_g1[j], rw_g2[j], rw_rk[j], rw_gn_g[j], rw_gn_b[j], rw_wo[j])
        else:
            yl = _fourier(hl, fn_wo[j], fn_bo[j])
            yc = _fourier(hc, fn_wo[j], fn_bo[j]) if ctx_full else None
        x = _ln_affine(alpha * x + (1.0 + m_l[2]) * yl, ln_g[i, 0], ln_b[i, 0])
        hl = _modulate(x, m_l[3], m_l[4])
        x = _ln_affine(alpha * x + (1.0 + m_l[5]) * _expert_choice_moe(hl, moe_router[i], moe_w1[i], moe_w3[i], moe_w2[i]),
                       ln_g[i, 1], ln_b[i, 1])
        if ctx_full:
            xc = _ln_affine(alpha * xc + (1.0 + m_c[2]) * yc, ln_g[i, 0], ln_b[i, 0])
            hc = _modulate(xc, m_c[3], m_c[4])
            xc = _ln_affine(alpha * xc + (1.0 + m_c[5]) * _expert_choice_moe(hc, moe_router[i], moe_w1[i], moe_w3[i], moe_w2[i]),
                            ln_g[i, 1], ln_b[i, 1])
    return x
```

```python
import functools
import math

import ml_dtypes
import numpy as np

import jax
import jax.numpy as jnp
from jax import lax
from jax.experimental import pallas as pl
from jax.experimental.pallas import tpu as pltpu

F32 = jnp.float32
BF16 = jnp.bfloat16

GRID_W = 64
N_MIXERS = 3
HY_ORDER = 2
HY_EMB_DIM = 33
HY_INNER_MLPS = 2
HY_DECAY_TARGET = 1e-2
HY_SHORT_DECAY_PCT = 0.3
HY_LONG_DECAY_PCT = 1.5
RW_HEAD_DIM = 64
RW_GN_EPS = 64e-5
FN_GROUPS = 8
N_EXPERTS = 16
EC_CAPACITY = 2
LN_EPS = 1e-5
ADALN_EPS = 1e-6

SUBLANES = 8
LANE_TILE = 128
SCAN_CHUNK = 64
SCAN_LANES = 1024
DFT_MINOR = 128
DFT_DC = 1024
FN_ROWS = 512
MOE_ROWS = 1024
MOE_F_TILE = 512
MOE_SCATTER_UNROLL = 4
MOE_VMEM_BYTES = 56 * 1024 * 1024

_NT = (((1,), (1,)), ((), ()))
_TN = (((0,), (0,)), ((), ()))


def _split3(x):
    hi = x.astype(BF16)
    r1 = x - hi.astype(F32)
    mid = r1.astype(BF16)
    lo = (r1 - mid.astype(F32)).astype(BF16)
    return hi, mid, lo


def _split2(x):
    hi = x.astype(BF16)
    return hi, (x - hi.astype(F32)).astype(BF16)


def _dot3(a, b, dims=(((1,), (0,)), ((), ()))):
    a_hi, a_lo = _split2(a)
    b_hi, b_lo = _split2(b)
    dg = functools.partial(lax.dot_general, dimension_numbers=dims, preferred_element_type=F32)
    return dg(a_hi, b_hi) + (dg(a_hi, b_lo) + dg(a_lo, b_hi))


def _lhs_parts(m):
    hi = m.astype(ml_dtypes.bfloat16)
    lo = (m - hi.astype(np.float64)).astype(ml_dtypes.bfloat16)
    return jnp.asarray(np.concatenate([hi, hi, lo], axis=1))


def _rhs_parts(x):
    hi, lo = _split2(x)
    return jnp.concatenate([hi, lo, hi], axis=0)


def _rhs_parts_t(x):
    hi, lo = _split2(x)
    return jnp.concatenate([hi, lo, hi], axis=1)


def _lane_tiled(t, width):
    return jnp.concatenate([t] * (width // t.shape[-1]), axis=-1)


def _layer_norm(x, eps):
    xf = x.astype(F32)
    mu = jnp.mean(xf, -1, keepdims=True)
    var = jnp.mean(jnp.square(xf - mu), -1, keepdims=True)
    return (xf - mu) * lax.rsqrt(var + eps)


def _ln_affine(x, g, b):
    return (_layer_norm(x, LN_EPS) * g.astype(F32) + b.astype(F32)).astype(x.dtype)


def _modulate(x, shift, scale):
    return (_layer_norm(x, ADALN_EPS) * (1.0 + scale.astype(F32)) + shift.astype(F32)).astype(x.dtype)


def _grid_pos_embed(rows, dim):
    r_idx = jnp.repeat(jnp.arange(rows, dtype=F32), GRID_W)
    c_idx = jnp.tile(jnp.arange(GRID_W, dtype=F32), rows)
    quarter = dim // 4
    omega = 1.0 / (10000.0 ** (jnp.arange(quarter, dtype=F32) / quarter))

    def emb(p):
        a = p[:, None] * omega[None, :]
        return jnp.concatenate([jnp.sin(a), jnp.cos(a)], -1)

    return jnp.concatenate([emb(r_idx), emb(c_idx)], -1)


def _centred_conv3(u, w, b):
    up = jnp.pad(u, ((0, 0), (1, 1), (0, 0)))
    return up[:, :-2] * w[0] + up[:, 1:-1] * w[1] + up[:, 2:] * w[2] + b


@functools.lru_cache(maxsize=None)
def _conv_dft_constants(n1, n_in):
    n = n1 * DFT_MINOR
    nk = n1 // 2 + 1
    k1 = np.arange(nk)
    ang = 2 * np.pi * np.outer(k1, np.arange(n_in)) / n1
    fwd = np.kron(np.concatenate([np.cos(ang), -np.sin(ang)], axis=0), np.eye(SUBLANES))
    phi = 2 * np.pi * np.outer(k1, np.arange(DFT_MINOR)) / n
    tw_c = jnp.asarray(np.repeat(np.cos(phi)[:, :, None], LANE_TILE, axis=2), F32)
    tw_s = jnp.asarray(np.repeat(np.sin(phi)[:, :, None], LANE_TILE, axis=2), F32)
    a2 = 2 * np.pi * np.outer(np.arange(DFT_MINOR), np.arange(DFT_MINOR)) / DFT_MINOR
    c2, s2 = np.cos(a2), np.sin(a2)
    minor_f = np.block([[c2, s2], [-s2, c2]])
    minor_i = np.block([[c2, -s2], [s2, c2]])
    w = np.where((k1 == 0) | (k1 == n1 // 2), 1.0, 2.0) / n
    ang_i = 2 * np.pi * np.outer(np.arange(n1 // 2), k1) / n1
    inv = np.kron(np.concatenate([np.cos(ang_i) * w, -np.sin(ang_i) * w], axis=1), np.eye(SUBLANES))
    return dict(nk=nk, fwd=_lhs_parts(fwd), tw_c=tw_c, tw_s=tw_s, minor_f=_lhs_parts(minor_f),
                minor_i=_lhs_parts(minor_i), inv=_lhs_parts(inv))


def _dft_major_fwd_kernel(g_ref, c_ref, s_ref, x_ref, o_ref):
    n_in, _, dc = x_ref.shape
    nk = c_ref.shape[0]
    xs = x_ref[...].reshape(n_in * SUBLANES, dc)
    a = jnp.dot(g_ref[...], _rhs_parts(xs), preferred_element_type=F32).reshape(2, nk, SUBLANES, dc)
    c = _lane_tiled(c_ref[...], dc)
    s = _lane_tiled(s_ref[...], dc)
    o_ref[0] = a[0] * c + a[1] * s
    o_ref[1] = a[1] * c - a[0] * s


def _dft_major_fwd(src, n_batch, batch_map, n_in, consts, width):
    bs, rows, dtot = src.shape
    nk = consts["nk"]
    view = src.reshape(bs, rows // DFT_MINOR, DFT_MINOR, dtot)
    dcb = width // DFT_DC
    g = consts["fwd"]

    def x_map(i, b, j):
        sb, lane_blk = batch_map(b)
        return (sb, 0, i, lane_blk * dcb + j)

    return pl.pallas_call(
        _dft_major_fwd_kernel,
        grid=(DFT_MINOR // SUBLANES, n_batch, dcb),
        in_specs=[pl.BlockSpec(g.shape, lambda i, b, j: (0, 0)),
                  pl.BlockSpec((nk, SUBLANES, LANE_TILE), lambda i, b, j: (0, i, 0)),
                  pl.BlockSpec((nk, SUBLANES, LANE_TILE), lambda i, b, j: (0, i, 0)),
                  pl.BlockSpec((None, n_in, SUBLANES, DFT_DC), x_map)],
        out_specs=pl.BlockSpec((None, 2, nk, SUBLANES, DFT_DC), lambda i, b, j: (b, 0, 0, i, j)),
        out_shape=jax.ShapeDtypeStruct((n_batch, 2, nk, DFT_MINOR, width), F32),
        compiler_params=pltpu.CompilerParams(dimension_semantics=("parallel", "parallel", "parallel")),
        name="dft_major_fwd",
    )(g, consts["tw_c"], consts["tw_s"], view)


def _dft_minor_fwd_kernel(mf_ref, a_ref, o_ref):
    dc = a_ref.shape[-1]
    a = a_ref[...].reshape(2 * DFT_MINOR, dc)
    o_ref[...] = jnp.dot(mf_ref[...], _rhs_parts(a), preferred_element_type=F32).reshape(2, DFT_MINOR, dc)


def _dft_minor_fwd(a, consts):
    n, _, nk, _, width = a.shape
    mf = consts["minor_f"]
    blk = pl.BlockSpec((None, 2, None, DFT_MINOR, DFT_DC), lambda k, j, b: (b, 0, k, 0, j))
    return pl.pallas_call(
        _dft_minor_fwd_kernel,
        grid=(nk, width // DFT_DC, n),
        in_specs=[pl.BlockSpec(mf.shape, lambda k, j, b: (0, 0)), blk],
        out_specs=blk,
        out_shape=jax.ShapeDtypeStruct(a.shape, F32),
        compiler_params=pltpu.CompilerParams(dimension_semantics=("parallel", "parallel", "parallel")),
        name="dft_minor_fwd",
    )(mf, a)


def _dft_minor_conv_kernel(mf_ref, mi_ref, kf_ref, a_ref, o_ref):
    dc = a_ref.shape[-1]
    a = a_ref[...].reshape(2 * DFT_MINOR, dc)
    x = jnp.dot(mf_ref[...], _rhs_parts(a), preferred_element_type=F32)
    xr, xi = x[:DFT_MINOR], x[DFT_MINOR:]
    kr, ki = kf_ref[0], kf_ref[1]
    y = jnp.concatenate([xr * kr - xi * ki, xr * ki + xi * kr], axis=0)
    o_ref[...] = jnp.dot(mi_ref[...], _rhs_parts(y), preferred_element_type=F32).reshape(2, DFT_MINOR, dc)


def _dft_minor_conv(a, kf, consts):
    n, _, nk, _, width = a.shape
    mf, mi = consts["minor_f"], consts["minor_i"]
    blk = pl.BlockSpec((None, 2, None, DFT_MINOR, DFT_DC), lambda k, j, b: (b, 0, k, 0, j))
    return pl.pallas_call(
        _dft_minor_conv_kernel,
        grid=(nk, width // DFT_DC, n),
        in_specs=[pl.BlockSpec(mf.shape, lambda k, j, b: (0, 0)),
                  pl.BlockSpec(mi.shape, lambda k, j, b: (0, 0)),
                  pl.BlockSpec((2, None, DFT_MINOR, DFT_DC), lambda k, j, b: (0, k, 0, j)),
                  blk],
        out_specs=blk,
        out_shape=jax.ShapeDtypeStruct(a.shape, F32),
        compiler_params=pltpu.CompilerParams(dimension_semantics=("parallel", "parallel", "parallel")),
        name="dft_minor_conv",
    )(mf, mi, kf, a)


def _dft_major_inv_gate_kernel(h_ref, c_ref, s_ref, bias_ref, b_ref, zin_ref, gate_ref, o_ref):
    _, nk, _, dc = b_ref.shape
    n_out = o_ref.shape[0]
    c = _lane_tiled(c_ref[...], dc)
    s = _lane_tiled(s_ref[...], dc)
    br, bi = b_ref[0], b_ref[1]
    bt = jnp.concatenate([br * c - bi * s, bi * c + br * s], axis=0).reshape(2 * nk * SUBLANES, dc)
    y = jnp.dot(h_ref[...], _rhs_parts(bt), preferred_element_type=F32).reshape(n_out, SUBLANES, dc)
    o_ref[...] = gate_ref[...] * (y + zin_ref[...] * bias_ref[...])


def _dft_major_inv_gate(bb, zsrc, zblk, gsrc, gblk, bias, consts):
    n, _, nk, _, width = bb.shape
    rows = zsrc.shape[1]
    n_out = rows // DFT_MINOR
    dcb = width // DFT_DC
    h = consts["inv"]
    zview = zsrc.reshape(n, n_out, DFT_MINOR, zsrc.shape[-1])
    gview = gsrc.reshape(n, n_out, DFT_MINOR, gsrc.shape[-1])
    tw = pl.BlockSpec((nk, SUBLANES, LANE_TILE), lambda i, b, j: (0, i, 0))
    out = pl.pallas_call(
        _dft_major_inv_gate_kernel,
        grid=(DFT_MINOR // SUBLANES, n, dcb),
        in_specs=[pl.BlockSpec(h.shape, lambda i, b, j: (0, 0)), tw, tw,
                  pl.BlockSpec((1, DFT_DC), lambda i, b, j: (0, j)),
                  pl.BlockSpec((None, 2, nk, SUBLANES, DFT_DC), lambda i, b, j: (b, 0, 0, i, j)),
                  pl.BlockSpec((None, n_out, SUBLANES, DFT_DC), lambda i, b, j: (b, 0, i, zblk * dcb + j)),
                  pl.BlockSpec((None, n_out, SUBLANES, DFT_DC), lambda i, b, j: (b, 0, i, gblk * dcb + j))],
        out_specs=pl.BlockSpec((None, n_out, SUBLANES, DFT_DC), lambda i, b, j: (b, 0, i, j)),
        out_shape=jax.ShapeDtypeStruct((n, n_out, DFT_MINOR, width), F32),
        compiler_params=pltpu.CompilerParams(dimension_semantics=("parallel", "parallel", "parallel")),
        name="dft_major_inv_gate",
    )(h, consts["tw_c"], consts["tw_s"], bias.reshape(1, width).astype(F32), bb, zview, gview)
    return out.reshape(n, rows, width)


def _hyena_two_sided(L, D, f_w1, f_b1, f_freq, f_w2, f_b2, f_w3):
    pos = jnp.arange(L, dtype=F32)
    t01 = pos / max(L - 1, 1)
    bands = (HY_EMB_DIM - 1) // 2
    f = jnp.linspace(1e-4, bands - 1, bands, dtype=F32)
    ang = f[None, :] * (2.0 * math.pi * pos / L)[:, None]
    z = jnp.concatenate([t01[:, None], jnp.cos(ang), -jnp.sin(ang)], -1)
    freq = f_freq.astype(F32)
    a = jnp.sin(freq * (z @ f_w1.astype(F32) + f_b1.astype(F32)))
    for n in range(HY_INNER_MLPS):
        a = jnp.sin(freq * (a @ f_w2[n].astype(F32) + f_b2[n].astype(F32)))
    h = (a @ f_w3.astype(F32)).reshape(L, 2, HY_ORDER, D)
    max_decay = math.log(HY_DECAY_TARGET) / HY_SHORT_DECAY_PCT
    min_decay = math.log(HY_DECAY_TARGET) / HY_LONG_DECAY_PCT
    deltas = jnp.linspace(min_decay, max_decay, D, dtype=F32)
    window = jnp.exp(-t01[:, None] * jnp.abs(deltas)[None, :])
    h = h * window[:, None, None, :]
    two_sided = jnp.concatenate([h[:, 0], jnp.zeros((1, HY_ORDER, D), F32), h[:0:-1, 1]], 0)
    return two_sided / jnp.sum(jnp.abs(two_sided), 0, keepdims=True)


def _hyena(h, w_in, b_in, conv_w, conv_b, f_w1, f_b1, f_freq, f_w2, f_b2, f_w3, f_bias, w_out, b_out):
    B_, L, D = h.shape
    n1 = 2 * L // DFT_MINOR
    u = _centred_conv3(h @ w_in + b_in, conv_w, conv_b)
    two_sided = _hyena_two_sided(L, D, f_w1, f_b1, f_freq, f_w2, f_b2, f_w3).reshape(1, 2 * L, HY_ORDER * D)
    kc = _conv_dft_constants(n1, n1)
    kf = _dft_minor_fwd(_dft_major_fwd(two_sided, HY_ORDER, lambda o: (0, o), n1, kc, D), kc)
    dc = _conv_dft_constants(n1, n1 // 2)

    def long_conv(zsrc, zblk, gblk, o):
        a = _dft_major_fwd(zsrc, B_, lambda b: (b, zblk), n1 // 2, dc, D)
        bb = _dft_minor_conv(a, kf[o], dc)
        return _dft_major_inv_gate(bb, zsrc, zblk, u, gblk, f_bias[o], dc)

    z = long_conv(u, 0, 1, 0)
    z = long_conv(z, 0, 2, 1)
    return z @ w_out + b_out


def _token_shift_bidir(h):
    hp = jnp.pad(h, ((0, 0), (1, 1), (0, 0)))
    return 0.5 * (hp[:, :-2] + hp[:, 2:]) - h


def _rwkv_prep(h, mu, wr, wk, wv, w0, w1, w2, a0, a1, a2, k_k, k_a):
    B_, T, D = h.shape
    H = D // RW_HEAD_DIM
    xx = _token_shift_bidir(h)
    xr, xw, xk, xv, xa, xg = [h + xx * mu[n] for n in range(6)]
    k = xk @ wk
    lw = jnp.einsum('nbtr,nrc->nbtc', jnp.tanh(jnp.einsum('btc,ncr->nbtr', xw, w1)), w2).astype(F32) + w0[:, None, None, :].astype(F32)
    log_decay = -math.exp(-0.5) * jax.nn.sigmoid(lw)
    a = jax.nn.sigmoid(jnp.einsum('nbtr,nrc->nbtc', jnp.einsum('btc,ncr->nbtr', xa, a1), a2).astype(F32) + a0[:, None, None, :].astype(F32))
    kk = (k * k_k).astype(F32).reshape(B_, T, H, RW_HEAD_DIM)
    kk = kk * lax.rsqrt(jnp.maximum(jnp.sum(kk * kk, -1, keepdims=True), 1e-24))
    kk = kk.reshape(B_, T, D)
    k_dir = k.astype(F32)[None] * (1.0 + (a - 1.0) * k_a.astype(F32))
    return {"r": (xr @ wr).astype(F32), "v": (xv @ wv).astype(F32), "kk": kk, "log_decay": log_decay,
            "k": k_dir, "kka": kk[None] * a, "xg": xg}


def _scan_kernel(r_ref, v_ref, kk_ref, lw_ref, k_ref, kka_ref, y_ref, s_ref):
    C = SCAN_CHUNK
    hd = RW_HEAD_DIM
    d = pl.program_id(0)

    @pl.when(pl.program_id(2) == 0)
    def _():
        s_ref[...] = jnp.zeros_like(s_ref)

    sign = 1 - 2 * d
    row = lax.broadcasted_iota(jnp.int32, (C, C), 0)
    col = lax.broadcasted_iota(jnp.int32, (C, C), 1)
    diff = (row - col) * sign
    incl = diff >= 0
    strict = diff > 0
    incl_b = incl.astype(BF16)
    eye = (row == col).astype(F32)
    heads_per_step = SCAN_LANES // hd

    def head_group(p, carry):
        sl = pl.ds(pl.multiple_of(p * SCAN_LANES, SCAN_LANES), SCAN_LANES)
        lw = lw_ref[0, 0, :, sl]
        cum = sum(jnp.dot(incl_b, part, preferred_element_type=F32) for part in _split3(lw))
        g_in = jnp.exp(cum)
        g_ex = jnp.exp(cum - lw)
        g_inv = jnp.exp(-cum)
        g_end = jnp.exp(jnp.sum(lw, axis=0, keepdims=True))
        a_t = -kk_ref[0, :, sl] * g_ex
        b_t = kka_ref[0, 0, :, sl] * g_inv
        k_t = k_ref[0, 0, :, sl] * g_inv
        r_t = r_ref[0, :, sl] * g_in
        v = v_ref[0, :, sl]
        hs = range(heads_per_step)
        cut = lambda t, hh: t[:, hh * hd:(hh + 1) * hd]
        a_h = [cut(a_t, hh) for hh in hs]
        r_h = [cut(r_t, hh) for hh in hs]
        v_h = [cut(v, hh) for hh in hs]
        bk = [jnp.concatenate([cut(b_t, hh), cut(k_t, hh)], axis=0) for hh in hs]
        sc = [_dot3(jnp.concatenate([a_h[hh], r_h[hh]], axis=0), bk[hh], _NT) for hh in hs]
        a_ab = [jnp.where(strict, s[:C, :C], 0.0) for s in sc]
        amk = [jnp.concatenate([jnp.where(strict, s[:C, C:], 0.0), jnp.where(incl, s[C:, C:], 0.0)], axis=0) for s in sc]
        m_rb = [jnp.where(incl, s[C:, :C], 0.0) for s in sc]
        av = [_dot3(amk[hh], v_h[hh]) for hh in hs]
        inv = [eye + a for a in a_ab]
        apow = [_dot3(a, a) for a in a_ab]
        n = 4
        while n < C:
            both = [_dot3(apow[hh], jnp.concatenate([inv[hh], apow[hh]], axis=1)) for hh in hs]
            inv = [inv[hh] + both[hh][:, :C] for hh in hs]
            apow = [both[hh][:, C:] for hh in hs]
            n *= 2
        inv = [inv[hh] + _dot3(apow[hh], inv[hh]) for hh in hs]
        wu = [_dot3(inv[hh], jnp.concatenate([a_h[hh], av[hh][:C]], axis=1)) for hh in hs]
        s0 = [s_ref[heads_per_step * p + hh] for hh in hs]
        ws = [_dot3(jnp.concatenate([wu[hh][:, :hd], r_h[hh]], axis=0), s0[hh], _NT) for hh in hs]
        u = [ws[hh][:C] + wu[hh][:, hd:] for hh in hs]
        s_add = [_dot3(jnp.concatenate([u[hh], v_h[hh]], axis=0), bk[hh], _TN) for hh in hs]
        for hh in hs:
            s_ref[heads_per_step * p + hh] = (s0[hh] + s_add[hh]) * cut(g_end, hh)
        ys = [ws[hh][C:] + _dot3(m_rb[hh], u[hh]) + av[hh][C:] for hh in hs]
        y_ref[0, 0, :, sl] = jnp.concatenate(ys, axis=1)
        return carry

    lax.fori_loop(0, lw_ref.shape[-1] // SCAN_LANES, head_group, 0)


def _wkv_scan(r, v, kk, log_decay, k_dir, kka, n_ctx):
    B_, T, D = r.shape
    C = SCAN_CHUNK
    assert n_ctx % C == 0 and T % C == 0 and D % SCAN_LANES == 0
    nc_ctx, nc = n_ctx // C, T // C

    def chunk(d, j):
        back = jnp.where(j < nc_ctx, nc_ctx - 1 - j, nc + nc_ctx - 1 - j)
        return jnp.where(d == 0, j, back)

    shared = pl.BlockSpec((1, C, D), lambda d, b, j: (b, chunk(d, j), 0))
    directed = pl.BlockSpec((1, 1, C, D), lambda d, b, j: (d, b, chunk(d, j), 0))
    return pl.pallas_call(
        _scan_kernel,
        grid=(2, B_, nc),
        in_specs=[shared, shared, shared, directed, directed, directed],
        out_specs=directed,
        out_shape=jax.ShapeDtypeStruct((2, B_, T, D), F32),
        scratch_shapes=[pltpu.VMEM((D // RW_HEAD_DIM, RW_HEAD_DIM, RW_HEAD_DIM), F32)],
        compiler_params=pltpu.CompilerParams(dimension_semantics=("parallel", "parallel", "arbitrary")),
        name="wkv_scan",
    )(r, v, kk, log_decay, k_dir, kka)


def _heads(t):
    return t.astype(F32).reshape(t.shape[:-1] + (t.shape[-1] // RW_HEAD_DIM, RW_HEAD_DIM))


def _rwkv_mixer(hc, hl, prep_params, g1, g2, r_k, gn_g, gn_b, w_o):
    pc = _rwkv_prep(hc, *prep_params)
    pl_ = _rwkv_prep(hl, *prep_params)
    n_ctx = hc.shape[1]

    def seq(name):
        return jnp.concatenate([pc[name], pl_[name]], axis=-2)

    y = _wkv_scan(seq("r"), seq("v"), seq("kk"), seq("log_decay"), seq("k"), seq("kka"), n_ctx)
    p, h = pl_, hl
    wkv = _heads(y[0, :, n_ctx:] + y[1, :, n_ctx:])
    m = jnp.mean(wkv, -1, keepdims=True)
    var = jnp.mean(jnp.square(wkv - m), -1, keepdims=True)
    gn = (wkv - m) * lax.rsqrt(var + RW_GN_EPS) * _heads(gn_g) + _heads(gn_b)
    bonus = jnp.sum(_heads(p["r"])[None] * _heads(p["k"]) * _heads(r_k), axis=(0, -1))[..., None] * _heads(p["v"])
    g = jax.nn.sigmoid(p["xg"] @ g1) @ g2
    o = (gn + bonus).reshape(h.shape).astype(h.dtype) * g
    return o @ w_o


@functools.lru_cache(maxsize=None)
def _fnet_dft_constants(n1, group):
    n = n1 * DFT_MINOR
    ang = 2 * np.pi * np.outer(np.arange(n1), np.arange(n1)) / n1
    c1, s1 = np.cos(ang), np.sin(ang)
    fwd = np.kron(np.block([[c1, s1], [-s1, c1]]), np.eye(SUBLANES))
    phi = 2 * np.pi * np.outer(np.arange(n1), np.arange(DFT_MINOR)) / n
    tw_c = jnp.asarray(np.repeat(np.cos(phi)[:, :, None], LANE_TILE, axis=2), F32)
    tw_s = jnp.asarray(np.repeat(np.sin(phi)[:, :, None], LANE_TILE, axis=2), F32)
    a2 = 2 * np.pi * np.outer(np.arange(DFT_MINOR), np.arange(DFT_MINOR)) / DFT_MINOR
    minor_re = np.concatenate([np.cos(a2), np.sin(a2)], axis=1) / math.sqrt(n * group)
    ag = 2 * np.pi * np.outer(np.arange(group), np.arange(group)) / group
    lane = np.concatenate([np.cos(ag), -np.sin(ag)], axis=1)
    lane_hi = lane.astype(ml_dtypes.bfloat16)
    lane_lo = (lane - lane_hi.astype(np.float64)).astype(ml_dtypes.bfloat16)
    lane_parts = jnp.asarray(np.concatenate([lane_hi, lane_hi, lane_lo], axis=0))
    return dict(nk=n1, fwd=_lhs_parts(fwd), tw_c=tw_c, tw_s=tw_s, minor_re=_lhs_parts(minor_re), lane=lane_parts)


def _lane_dft_kernel(f_ref, x_ref, o_ref):
    group = f_ref.shape[1] // 2
    for g in range(x_ref.shape[-1] // group):
        ls = slice(g * group, (g + 1) * group)
        z = jnp.dot(_rhs_parts_t(x_ref[:, ls]), f_ref[...], preferred_element_type=F32)
        o_ref[0, :, ls] = z[:, :group]
        o_ref[1, :, ls] = z[:, group:]


def _lane_dft(x, consts, group):
    B_, T, D = x.shape
    f = consts["lane"]
    return pl.pallas_call(
        _lane_dft_kernel,
        grid=(B_, T // FN_ROWS),
        in_specs=[pl.BlockSpec(f.shape, lambda b, t: (0, 0)),
                  pl.BlockSpec((None, FN_ROWS, D), lambda b, t: (b, t, 0))],
        out_specs=pl.BlockSpec((None, 2, FN_ROWS, D), lambda b, t: (b, 0, t, 0)),
        out_shape=jax.ShapeDtypeStruct((B_, 2, T, D), F32),
        compiler_params=pltpu.CompilerParams(dimension_semantics=("parallel", "parallel")),
        name="fnet_lane_dft",
    )(f, x)


def _dft_minor_real_kernel(m_ref, a_ref, o_ref):
    for q in range(SUBLANES):
        a = jnp.concatenate([a_ref[0, q], a_ref[1, q]], axis=0)
        o_ref[:, q, :] = jnp.dot(m_ref[...], _rhs_parts(a), preferred_element_type=F32)


def _dft_minor_real(a, consts):
    n, _, n1, _, width = a.shape
    m = consts["minor_re"]
    out = pl.pallas_call(
        _dft_minor_real_kernel,
        grid=(n1 // SUBLANES, width // DFT_DC, n),
        in_specs=[pl.BlockSpec(m.shape, lambda k, j, b: (0, 0)),
                  pl.BlockSpec((None, 2, SUBLANES, DFT_MINOR, DFT_DC), lambda k, j, b: (b, 0, k, 0, j))],
        out_specs=pl.BlockSpec((None, DFT_MINOR, SUBLANES, DFT_DC), lambda k, j, b: (b, 0, k, j)),
        out_shape=jax.ShapeDtypeStruct((n, DFT_MINOR, n1, width), F32),
        compiler_params=pltpu.CompilerParams(dimension_semantics=("parallel", "parallel", "parallel")),
        name="dft_minor_real",
    )(m, a)
    return out.reshape(n, DFT_MINOR * n1, width)


def _fourier(h, w_o, b_o):
    B_, T, D = h.shape
    n1 = T // DFT_MINOR
    group = D // FN_GROUPS
    consts = _fnet_dft_constants(n1, group)
    z = _lane_dft(h.astype(F32), consts, group).reshape(B_, 2 * T, D)
    a = _dft_major_fwd(z, B_, lambda b: (b, 0), 2 * n1, consts, D)
    mixed = _dft_minor_real(a, consts)
    return mixed.astype(h.dtype) @ w_o + b_o


def _moe_ffn_kernel(x_ref, g_ref, w1_ref, w3_ref, w2_ref, o_ref, xb_ref):
    f = pl.program_id(2)

    @pl.when(f == 0)
    def _():
        xb_ref[...] = x_ref[...].reshape(xb_ref.shape).astype(BF16)

    xb = xb_ref[...]
    h1 = jnp.dot(xb, w1_ref[...].astype(BF16), preferred_element_type=F32)
    h3 = jnp.dot(xb, w3_ref[...].astype(BF16), preferred_element_type=F32)
    he = (h1 * jax.nn.sigmoid(h1) * h3).astype(BF16)
    part = jnp.dot(he, w2_ref[...].astype(BF16), preferred_element_type=F32).reshape(o_ref.shape)

    @pl.when(f == 0)
    def _():
        o_ref[...] = part

    @pl.when(f > 0)
    def _():
        o_ref[...] += part

    @pl.when(f == pl.num_programs(2) - 1)
    def _():
        o_ref[...] *= g_ref[...]


def _moe_ffn(xe, gate, w1, w3, w2):
    B_, E, cap, D = xe.shape
    F = w1.shape[-1]
    mb = max(1, min(B_, MOE_ROWS // cap))
    ft = min(F, MOE_F_TILE)
    assert B_ % mb == 0 and F % ft == 0
    tok = pl.BlockSpec((mb, None, cap, D), lambda e, m, f: (m, e, 0, 0))
    return pl.pallas_call(
        _moe_ffn_kernel,
        grid=(E, B_ // mb, F // ft),
        in_specs=[tok,
                  pl.BlockSpec((mb, None, cap, 1), lambda e, m, f: (m, e, 0, 0)),
                  pl.BlockSpec((None, D, ft), lambda e, m, f: (e, 0, f)),
                  pl.BlockSpec((None, D, ft), lambda e, m, f: (e, 0, f)),
                  pl.BlockSpec((None, ft, D), lambda e, m, f: (e, f, 0))],
        out_specs=tok,
        out_shape=jax.ShapeDtypeStruct((B_, E, cap, D), F32),
        scratch_shapes=[pltpu.VMEM((mb * cap, D), BF16)],
        compiler_params=pltpu.CompilerParams(dimension_semantics=("parallel", "parallel", "arbitrary"),
                                             vmem_limit_bytes=MOE_VMEM_BYTES),
        name="moe_ffn",
    )(xe, gate[..., None].astype(F32), w1, w3, w2)


def _moe_scatter_kernel(idx_ref, ye_ref, buf_ref, o_ref):
    del buf_ref
    e = pl.program_id(0)
    cap = ye_ref.shape[0]

    @pl.when(e == 0)
    def _():
        o_ref[...] = jnp.zeros_like(o_ref)

    def group(g, carry):
        rows = []
        for u in range(MOE_SCATTER_UNROLL):
            j = g * MOE_SCATTER_UNROLL + u
            t = idx_ref[e * cap + j]
            rows.append((t, o_ref[pl.ds(t, 1), :] + ye_ref[pl.ds(j, 1), :]))
        for t, r in rows:
            o_ref[pl.ds(t, 1), :] = r
        return carry

    lax.fori_loop(0, cap // MOE_SCATTER_UNROLL, group, 0)


def _moe_scatter(ye, idx, T):
    B_, E, cap, D = ye.shape
    assert cap % MOE_SCATTER_UNROLL == 0
    out = jnp.zeros((B_, T, D), F32)
    for b in range(B_):
        out = pl.pallas_call(
            _moe_scatter_kernel,
            grid_spec=pltpu.PrefetchScalarGridSpec(
                num_scalar_prefetch=1,
                grid=(E,),
                in_specs=[pl.BlockSpec((None, None, cap, D), lambda e, idx_ref, b=b: (b, e, 0, 0)),
                          pl.BlockSpec(memory_space=pl.ANY)],
                out_specs=pl.BlockSpec((None, T, D), lambda e, idx_ref, b=b: (b, 0, 0)),
            ),
            out_shape=jax.ShapeDtypeStruct((B_, T, D), F32),
            input_output_aliases={2: 0},
            compiler_params=pltpu.CompilerParams(dimension_semantics=("arbitrary",),
                                                 vmem_limit_bytes=MOE_VMEM_BYTES),
            name="moe_scatter",
        )(idx[b].reshape(E * cap).astype(jnp.int32), ye, out)
    return out


def _expert_choice_moe(h, w_router, w1, w3, w2):
    B_, T, _ = h.shape
    cap = EC_CAPACITY * T // N_EXPERTS
    aff = jax.nn.softmax((h @ w_router).astype(F32), -1)
    gate, idx = lax.top_k(jnp.swapaxes(aff, 1, 2), cap)
    bidx = jnp.arange(B_)[:, None, None]
    xe = h[bidx, idx]
    ye = _moe_ffn(xe, gate, w1, w3, w2)
    return _moe_scatter(ye, idx, T).astype(h.dtype)


def kernel(x, c, ctx, c_ctx, mod_w, mod_b, ln_g, ln_b, moe_router, moe_w1, moe_w3, moe_w2, hy_w_in, hy_b_in, hy_conv_w, hy_conv_b, hy_f_w1, hy_f_b1, hy_f_freq, hy_f_w2, hy_f_b2, hy_f_w3, hy_f_bias, hy_w_out, hy_b_out, rw_mu, rw_wr, rw_wk, rw_wv, rw_w0, rw_w1, rw_w2, rw_a0, rw_a1, rw_a2, rw_kk, rw_ka, rw_g1, rw_g2, rw_rk, rw_gn_g, rw_gn_b, rw_wo, fn_wo, fn_bo):
    depth = mod_w.shape[0]
    D = x.shape[-1]
    alpha = (2 * depth) ** 0.25
    n_lat = x.shape[1]
    x = x + _grid_pos_embed(n_lat // GRID_W, D).astype(x.dtype)[None]
    xc = ctx
    readers = [i for i in range(depth) if i % N_MIXERS == 1]
    last_reader = readers[-1] if readers else -1
    for i in range(depth):
        kind, j = i % N_MIXERS, i // N_MIXERS
        ctx_in = i <= last_reader
        ctx_full = i < last_reader
        m_l = jnp.split((jax.nn.silu(c) @ mod_w[i] + mod_b[i])[:, None, :], 6, -1)
        hl = _modulate(x, m_l[0], m_l[1])
        if ctx_in:
            m_c = jnp.split((jax.nn.silu(c_ctx) @ mod_w[i] + mod_b[i])[None, None, :], 6, -1)
            hc = _modulate(xc, m_c[0], m_c[1])
        if kind == 0:
            hy = (hy_w_in[j], hy_b_in[j], hy_conv_w[j], hy_conv_b[j], hy_f_w1[j], hy_f_b1[j], hy_f_freq[j],
                  hy_f_w2[j], hy_f_b2[j], hy_f_w3[j], hy_f_bias[j], hy_w_out[j], hy_b_out[j])
            yl = _hyena(hl, *hy)
            yc = _hyena(hc, *hy) if ctx_full else None
        elif kind == 1:
            prep = (rw_mu[j], rw_wr[j], rw_wk[j], rw_wv[j], rw_w0[j], rw_w1[j], rw_w2[j],
                    rw_a0[j], rw_a1[j], rw_a2[j], rw_kk[j], rw_ka[j])
            assert not ctx_full
            yc = None
            yl = _rwkv_mixer(hc, hl, prep, rw_g1[j], rw_g2[j], rw_rk[j], rw_gn_g[j], rw_gn_b[j], rw_wo[j])
        else:
            yl = _fourier(hl, fn_wo[j], fn_bo[j])
            yc = _fourier(hc, fn_wo[j], fn_bo[j]) if ctx_full else None
        x = _ln_affine(alpha * x + (1.0 + m_l[2]) * yl, ln_g[i, 0], ln_b[i, 0])
        hl = _modulate(x, m_l[3], m_l[4])
        x = _ln_affine(alpha * x + (1.0 + m_l[5]) * _expert_choice_moe(hl, moe_router[i], moe_w1[i], moe_w3[i], moe_w2[i]),
                       ln_g[i, 1], ln_b[i, 1])
        if ctx_full:
            xc = _ln_affine(alpha * xc + (1.0 + m_c[2]) * yc, ln_g[i, 0], ln_b[i, 0])
            hc = _modulate(xc, m_c[3], m_c[4])
            xc = _ln_affine(alpha * xc + (1.0 + m_c[5]) * _expert_choice_moe(hc, moe_router[i], moe_w1[i], moe_w3[i], moe_w2[i]),
                            ln_g[i, 1], ln_b[i, 1])
    return x
```

```python
import functools
import math

import ml_dtypes
import numpy as np

import jax
import jax.numpy as jnp
from jax import lax
from jax.experimental import pallas as pl
from jax.experimental.pallas import tpu as pltpu

F32 = jnp.float32
BF16 = jnp.bfloat16

GRID_W = 64
N_MIXERS = 3
HY_ORDER = 2
HY_EMB_DIM = 33
HY_INNER_MLPS = 2
HY_DECAY_TARGET = 1e-2
HY_SHORT_DECAY_PCT = 0.3
HY_LONG_DECAY_PCT = 1.5
RW_HEAD_DIM = 64
RW_GN_EPS = 64e-5
FN_GROUPS = 8
N_EXPERTS = 16
EC_CAPACITY = 2
LN_EPS = 1e-5
ADALN_EPS = 1e-6

SUBLANES = 8
LANE_TILE = 128
SCAN_CHUNK = 64
SCAN_LANES = 1024
DFT_MINOR = 128
DFT_DC = 1024
FN_ROWS = 512
MOE_ROWS = 1024
MOE_F_TILE = 512
MOE_SCATTER_UNROLL = 4
MOE_VMEM_BYTES = 56 * 1024 * 1024

_NT = (((1,), (1,)), ((), ()))
_TN = (((0,), (0,)), ((), ()))


def _split3(x):
    hi = x.astype(BF16)
    r1 = x - hi.astype(F32)
    mid = r1.astype(BF16)
    lo = (r1 - mid.astype(F32)).astype(BF16)
    return hi, mid, lo


def _split2(x):
    hi = x.astype(BF16)
    return hi, (x - hi.astype(F32)).astype(BF16)


def _dot3(a, b, dims=(((1,), (0,)), ((), ()))):
    a_hi, a_lo = a if isinstance(a, tuple) else _split2(a)
    b_hi, b_lo = b if isinstance(b, tuple) else _split2(b)
    dg = functools.partial(lax.dot_general, dimension_numbers=dims, preferred_element_type=F32)
    return dg(a_hi, b_hi) + (dg(a_hi, b_lo) + dg(a_lo, b_hi))


def _lhs_parts(m):
    hi = m.astype(ml_dtypes.bfloat16)
    lo = (m - hi.astype(np.float64)).astype(ml_dtypes.bfloat16)
    return jnp.asarray(np.concatenate([hi, hi, lo], axis=1))


def _rhs_parts(x):
    hi, lo = _split2(x)
    return jnp.concatenate([hi, lo, hi], axis=0)


def _rhs_parts_t(x):
    hi, lo = _split2(x)
    return jnp.concatenate([hi, lo, hi], axis=1)


def _lane_tiled(t, width):
    return jnp.concatenate([t] * (width // t.shape[-1]), axis=-1)


def _layer_norm(x, eps):
    xf = x.astype(F32)
    mu = jnp.mean(xf, -1, keepdims=True)
    var = jnp.mean(jnp.square(xf - mu), -1, keepdims=True)
    return (xf - mu) * lax.rsqrt(var + eps)


def _ln_affine(x, g, b):
    return (_layer_norm(x, LN_EPS) * g.astype(F32) + b.astype(F32)).astype(x.dtype)


def _modulate(x, shift, scale):
    return (_layer_norm(x, ADALN_EPS) * (1.0 + scale.astype(F32)) + shift.astype(F32)).astype(x.dtype)


def _grid_pos_embed(rows, dim):
    r_idx = jnp.repeat(jnp.arange(rows, dtype=F32), GRID_W)
    c_idx = jnp.tile(jnp.arange(GRID_W, dtype=F32), rows)
    quarter = dim // 4
    omega = 1.0 / (10000.0 ** (jnp.arange(quarter, dtype=F32) / quarter))

    def emb(p):
        a = p[:, None] * omega[None, :]
        return jnp.concatenate([jnp.sin(a), jnp.cos(a)], -1)

    return jnp.concatenate([emb(r_idx), emb(c_idx)], -1)


def _centred_conv3(u, w, b):
    up = jnp.pad(u, ((0, 0), (1, 1), (0, 0)))
    return up[:, :-2] * w[0] + up[:, 1:-1] * w[1] + up[:, 2:] * w[2] + b


@functools.lru_cache(maxsize=None)
def _conv_dft_constants(n1, n_in):
    n = n1 * DFT_MINOR
    nk = n1 // 2 + 1
    k1 = np.arange(nk)
    ang = 2 * np.pi * np.outer(k1, np.arange(n_in)) / n1
    fwd = np.kron(np.concatenate([np.cos(ang), -np.sin(ang)], axis=0), np.eye(SUBLANES))
    phi = 2 * np.pi * np.outer(k1, np.arange(DFT_MINOR)) / n
    tw_c = jnp.asarray(np.repeat(np.cos(phi)[:, :, None], LANE_TILE, axis=2), F32)
    tw_s = jnp.asarray(np.repeat(np.sin(phi)[:, :, None], LANE_TILE, axis=2), F32)
    a2 = 2 * np.pi * np.outer(np.arange(DFT_MINOR), np.arange(DFT_MINOR)) / DFT_MINOR
    c2, s2 = np.cos(a2), np.sin(a2)
    minor_f = np.block([[c2, s2], [-s2, c2]])
    minor_i = np.block([[c2, -s2], [s2, c2]])
    w = np.where((k1 == 0) | (k1 == n1 // 2), 1.0, 2.0) / n
    ang_i = 2 * np.pi * np.outer(np.arange(n1 // 2), k1) / n1
    inv = np.kron(np.concatenate([np.cos(ang_i) * w, -np.sin(ang_i) * w], axis=1), np.eye(SUBLANES))
    return dict(nk=nk, fwd=_lhs_parts(fwd), tw_c=tw_c, tw_s=tw_s, minor_f=_lhs_parts(minor_f),
                minor_i=_lhs_parts(minor_i), inv=_lhs_parts(inv))


def _dft_major_fwd_kernel(g_ref, c_ref, s_ref, x_ref, o_ref):
    n_in, _, dc = x_ref.shape
    nk = c_ref.shape[0]
    xs = x_ref[...].reshape(n_in * SUBLANES, dc)
    a = jnp.dot(g_ref[...], _rhs_parts(xs), preferred_element_type=F32).reshape(2, nk, SUBLANES, dc)
    c = _lane_tiled(c_ref[...], dc)
    s = _lane_tiled(s_ref[...], dc)
    o_ref[0] = a[0] * c + a[1] * s
    o_ref[1] = a[1] * c - a[0] * s


def _dft_major_fwd(src, n_batch, batch_map, n_in, consts, width):
    bs, rows, dtot = src.shape
    nk = consts["nk"]
    view = src.reshape(bs, rows // DFT_MINOR, DFT_MINOR, dtot)
    dcb = width // DFT_DC
    g = consts["fwd"]

    def x_map(i, b, j):
        sb, lane_blk = batch_map(b)
        return (sb, 0, i, lane_blk * dcb + j)

    return pl.pallas_call(
        _dft_major_fwd_kernel,
        grid=(DFT_MINOR // SUBLANES, n_batch, dcb),
        in_specs=[pl.BlockSpec(g.shape, lambda i, b, j: (0, 0)),
                  pl.BlockSpec((nk, SUBLANES, LANE_TILE), lambda i, b, j: (0, i, 0)),
                  pl.BlockSpec((nk, SUBLANES, LANE_TILE), lambda i, b, j: (0, i, 0)),
                  pl.BlockSpec((None, n_in, SUBLANES, DFT_DC), x_map)],
        out_specs=pl.BlockSpec((None, 2, nk, SUBLANES, DFT_DC), lambda i, b, j: (b, 0, 0, i, j)),
        out_shape=jax.ShapeDtypeStruct((n_batch, 2, nk, DFT_MINOR, width), F32),
        compiler_params=pltpu.CompilerParams(dimension_semantics=("parallel", "parallel", "parallel")),
        name="dft_major_fwd",
    )(g, consts["tw_c"], consts["tw_s"], view)


def _dft_minor_fwd_kernel(mf_ref, a_ref, o_ref):
    dc = a_ref.shape[-1]
    a = a_ref[...].reshape(2 * DFT_MINOR, dc)
    o_ref[...] = jnp.dot(mf_ref[...], _rhs_parts(a), preferred_element_type=F32).reshape(2, DFT_MINOR, dc)


def _dft_minor_fwd(a, consts):
    n, _, nk, _, width = a.shape
    mf = consts["minor_f"]
    blk = pl.BlockSpec((None, 2, None, DFT_MINOR, DFT_DC), lambda k, j, b: (b, 0, k, 0, j))
    return pl.pallas_call(
        _dft_minor_fwd_kernel,
        grid=(nk, width // DFT_DC, n),
        in_specs=[pl.BlockSpec(mf.shape, lambda k, j, b: (0, 0)), blk],
        out_specs=blk,
        out_shape=jax.ShapeDtypeStruct(a.shape, F32),
        compiler_params=pltpu.CompilerParams(dimension_semantics=("parallel", "parallel", "parallel")),
        name="dft_minor_fwd",
    )(mf, a)


def _dft_minor_conv_kernel(mf_ref, mi_ref, kf_ref, a_ref, o_ref):
    dc = a_ref.shape[-1]
    a = a_ref[...].reshape(2 * DFT_MINOR, dc)
    x = jnp.dot(mf_ref[...], _rhs_parts(a), preferred_element_type=F32)
    xr, xi = x[:DFT_MINOR], x[DFT_MINOR:]
    kr, ki = kf_ref[0], kf_ref[1]
    y = jnp.concatenate([xr * kr - xi * ki, xr * ki + xi * kr], axis=0)
    o_ref[...] = jnp.dot(mi_ref[...], _rhs_parts(y), preferred_element_type=F32).reshape(2, DFT_MINOR, dc)


def _dft_minor_conv(a, kf, consts):
    n, _, nk, _, width = a.shape
    mf, mi = consts["minor_f"], consts["minor_i"]
    blk = pl.BlockSpec((None, 2, None, DFT_MINOR, DFT_DC), lambda k, j, b: (b, 0, k, 0, j))
    return pl.pallas_call(
        _dft_minor_conv_kernel,
        grid=(nk, width // DFT_DC, n),
        in_specs=[pl.BlockSpec(mf.shape, lambda k, j, b: (0, 0)),
                  pl.BlockSpec(mi.shape, lambda k, j, b: (0, 0)),
                  pl.BlockSpec((2, None, DFT_MINOR, DFT_DC), lambda k, j, b: (0, k, 0, j)),
                  blk],
        out_specs=blk,
        out_shape=jax.ShapeDtypeStruct(a.shape, F32),
        compiler_params=pltpu.CompilerParams(dimension_semantics=("parallel", "parallel", "parallel")),
        name="dft_minor_conv",
    )(mf, mi, kf, a)


def _dft_major_inv_gate_kernel(h_ref, c_ref, s_ref, bias_ref, b_ref, zin_ref, gate_ref, o_ref):
    _, nk, _, dc = b_ref.shape
    n_out = o_ref.shape[0]
    c = _lane_tiled(c_ref[...], dc)
    s = _lane_tiled(s_ref[...], dc)
    br, bi = b_ref[0], b_ref[1]
    bt = jnp.concatenate([br * c - bi * s, bi * c + br * s], axis=0).reshape(2 * nk * SUBLANES, dc)
    y = jnp.dot(h_ref[...], _rhs_parts(bt), preferred_element_type=F32).reshape(n_out, SUBLANES, dc)
    o_ref[...] = gate_ref[...] * (y + zin_ref[...] * bias_ref[...])


def _dft_major_inv_gate(bb, zsrc, zblk, gsrc, gblk, bias, consts):
    n, _, nk, _, width = bb.shape
    rows = zsrc.shape[1]
    n_out = rows // DFT_MINOR
    dcb = width // DFT_DC
    h = consts["inv"]
    zview = zsrc.reshape(n, n_out, DFT_MINOR, zsrc.shape[-1])
    gview = gsrc.reshape(n, n_out, DFT_MINOR, gsrc.shape[-1])
    tw = pl.BlockSpec((nk, SUBLANES, LANE_TILE), lambda i, b, j: (0, i, 0))
    out = pl.pallas_call(
        _dft_major_inv_gate_kernel,
        grid=(DFT_MINOR // SUBLANES, n, dcb),
        in_specs=[pl.BlockSpec(h.shape, lambda i, b, j: (0, 0)), tw, tw,
                  pl.BlockSpec((1, DFT_DC), lambda i, b, j: (0, j)),
                  pl.BlockSpec((None, 2, nk, SUBLANES, DFT_DC), lambda i, b, j: (b, 0, 0, i, j)),
                  pl.BlockSpec((None, n_out, SUBLANES, DFT_DC), lambda i, b, j: (b, 0, i, zblk * dcb + j)),
                  pl.BlockSpec((None, n_out, SUBLANES, DFT_DC), lambda i, b, j: (b, 0, i, gblk * dcb + j))],
        out_specs=pl.BlockSpec((None, n_out, SUBLANES, DFT_DC), lambda i, b, j: (b, 0, i, j)),
        out_shape=jax.ShapeDtypeStruct((n, n_out, DFT_MINOR, width), F32),
        compiler_params=pltpu.CompilerParams(dimension_semantics=("parallel", "parallel", "parallel")),
        name="dft_major_inv_gate",
    )(h, consts["tw_c"], consts["tw_s"], bias.reshape(1, width).astype(F32), bb, zview, gview)
    return out.reshape(n, rows, width)


def _hyena_two_sided(L, D, f_w1, f_b1, f_freq, f_w2, f_b2, f_w3):
    pos = jnp.arange(L, dtype=F32)
    t01 = pos / max(L - 1, 1)
    bands = (HY_EMB_DIM - 1) // 2
    f = jnp.linspace(1e-4, bands - 1, bands, dtype=F32)
    ang = f[None, :] * (2.0 * math.pi * pos / L)[:, None]
    z = jnp.concatenate([t01[:, None], jnp.cos(ang), -jnp.sin(ang)], -1)
    freq = f_freq.astype(F32)
    a = jnp.sin(freq * (z @ f_w1.astype(F32) + f_b1.astype(F32)))
    for n in range(HY_INNER_MLPS):
        a = jnp.sin(freq * (a @ f_w2[n].astype(F32) + f_b2[n].astype(F32)))
    h = (a @ f_w3.astype(F32)).reshape(L, 2, HY_ORDER, D)
    max_decay = math.log(HY_DECAY_TARGET) / HY_SHORT_DECAY_PCT
    min_decay = math.log(HY_DECAY_TARGET) / HY_LONG_DECAY_PCT
    deltas = jnp.linspace(min_decay, max_decay, D, dtype=F32)
    window = jnp.exp(-t01[:, None] * jnp.abs(deltas)[None, :])
    h = h * window[:, None, None, :]
    two_sided = jnp.concatenate([h[:, 0], jnp.zeros((1, HY_ORDER, D), F32), h[:0:-1, 1]], 0)
    return two_sided / jnp.sum(jnp.abs(two_sided), 0, keepdims=True)


def _hyena(h, w_in, b_in, conv_w, conv_b, f_w1, f_b1, f_freq, f_w2, f_b2, f_w3, f_bias, w_out, b_out):
    B_, L, D = h.shape
    n1 = 2 * L // DFT_MINOR
    u = _centred_conv3(h @ w_in + b_in, conv_w, conv_b)
    two_sided = _hyena_two_sided(L, D, f_w1, f_b1, f_freq, f_w2, f_b2, f_w3).reshape(1, 2 * L, HY_ORDER * D)
    kc = _conv_dft_constants(n1, n1)
    kf = _dft_minor_fwd(_dft_major_fwd(two_sided, HY_ORDER, lambda o: (0, o), n1, kc, D), kc)
    dc = _conv_dft_constants(n1, n1 // 2)

    def long_conv(zsrc, zblk, gblk, o):
        a = _dft_major_fwd(zsrc, B_, lambda b: (b, zblk), n1 // 2, dc, D)
        bb = _dft_minor_conv(a, kf[o], dc)
        return _dft_major_inv_gate(bb, zsrc, zblk, u, gblk, f_bias[o], dc)

    z = long_conv(u, 0, 1, 0)
    z = long_conv(z, 0, 2, 1)
    return z @ w_out + b_out


def _token_shift_bidir(h):
    hp = jnp.pad(h, ((0, 0), (1, 1), (0, 0)))
    return 0.5 * (hp[:, :-2] + hp[:, 2:]) - h


def _rwkv_prep(h, mu, wr, wk, wv, w0, w1, w2, a0, a1, a2, k_k, k_a):
    B_, T, D = h.shape
    H = D // RW_HEAD_DIM
    xx = _token_shift_bidir(h)
    xr, xw, xk, xv, xa, xg = [h + xx * mu[n] for n in range(6)]
    k = xk @ wk
    lw = jnp.einsum('nbtr,nrc->nbtc', jnp.tanh(jnp.einsum('btc,ncr->nbtr', xw, w1)), w2).astype(F32) + w0[:, None, None, :].astype(F32)
    log_decay = -math.exp(-0.5) * jax.nn.sigmoid(lw)
    a = jax.nn.sigmoid(jnp.einsum('nbtr,nrc->nbtc', jnp.einsum('btc,ncr->nbtr', xa, a1), a2).astype(F32) + a0[:, None, None, :].astype(F32))
    kk = (k * k_k).astype(F32).reshape(B_, T, H, RW_HEAD_DIM)
    kk = kk * lax.rsqrt(jnp.maximum(jnp.sum(kk * kk, -1, keepdims=True), 1e-24))
    kk = kk.reshape(B_, T, D)
    k_dir = k.astype(F32)[None] * (1.0 + (a - 1.0) * k_a.astype(F32))
    return {"r": (xr @ wr).astype(F32), "v": (xv @ wv).astype(F32), "kk": kk, "log_decay": log_decay,
            "k": k_dir, "kka": kk[None] * a, "xg": xg}


def _scan_kernel(r_ref, v_ref, kk_ref, lw_ref, k_ref, kka_ref, y_ref, s_ref):
    C = SCAN_CHUNK
    hd = RW_HEAD_DIM
    d = pl.program_id(0)

    @pl.when(pl.program_id(2) == 0)
    def _():
        s_ref[...] = jnp.zeros_like(s_ref)

    sign = 1 - 2 * d
    row = lax.broadcasted_iota(jnp.int32, (C, C), 0)
    col = lax.broadcasted_iota(jnp.int32, (C, C), 1)
    diff = (row - col) * sign
    incl = diff >= 0
    strict = diff > 0
    incl_b = incl.astype(BF16)
    eye = (row == col).astype(F32)
    heads_per_step = SCAN_LANES // hd

    def head_group(p, carry):
        sl = pl.ds(pl.multiple_of(p * SCAN_LANES, SCAN_LANES), SCAN_LANES)
        lw = lw_ref[0, 0, :, sl]
        cum = sum(jnp.dot(incl_b, part, preferred_element_type=F32) for part in _split3(lw))
        g_in = jnp.exp(cum)
        g_ex = jnp.exp(cum - lw)
        g_inv = jnp.exp(-cum)
        g_end = jnp.exp(jnp.sum(lw, axis=0, keepdims=True))
        a_t = _split2(-kk_ref[0, :, sl] * g_ex)
        b_t = _split2(kka_ref[0, 0, :, sl] * g_inv)
        k_t = _split2(k_ref[0, 0, :, sl] * g_inv)
        r_t = _split2(r_ref[0, :, sl] * g_in)
        v = _split2(v_ref[0, :, sl])
        hs = range(heads_per_step)
        cut = lambda t, hh: t[:, hh * hd:(hh + 1) * hd]
        cutp = lambda p, hh: (cut(p[0], hh), cut(p[1], hh))
        catp = lambda ps, axis: tuple(jnp.concatenate([p[i] for p in ps], axis=axis) for i in range(2))
        a_h = [cutp(a_t, hh) for hh in hs]
        r_h = [cutp(r_t, hh) for hh in hs]
        v_h = [cutp(v, hh) for hh in hs]
        bk = [catp([cutp(b_t, hh), cutp(k_t, hh)], 0) for hh in hs]
        sc = [_dot3(catp([a_h[hh], r_h[hh]], 0), bk[hh], _NT) for hh in hs]
        a_ab = [jnp.where(strict, s[:C, :C], 0.0) for s in sc]
        amk = [jnp.concatenate([jnp.where(strict, s[:C, C:], 0.0), jnp.where(incl, s[C:, C:], 0.0)], axis=0) for s in sc]
        m_rb = [jnp.where(incl, s[C:, :C], 0.0) for s in sc]
        av = [_dot3(amk[hh], v_h[hh]) for hh in hs]
        inv = [eye + a for a in a_ab]
        ap = [_split2(a) for a in a_ab]
        apow = [_dot3(p, p) for p in ap]
        n = 4
        while n < C:
            ap = [_split2(p) for p in apow]
            both = [_dot3(ap[hh], catp([_split2(inv[hh]), ap[hh]], 1)) for hh in hs]
            inv = [inv[hh] + both[hh][:, :C] for hh in hs]
            apow = [both[hh][:, C:] for hh in hs]
            n *= 2
        inv = [inv[hh] + _dot3(apow[hh], inv[hh]) for hh in hs]
        wu = [_dot3(inv[hh], catp([a_h[hh], _split2(av[hh][:C])], 1)) for hh in hs]
        s0 = [s_ref[heads_per_step * p + hh] for hh in hs]
        ws = [_dot3(catp([_split2(wu[hh][:, :hd]), r_h[hh]], 0), s0[hh], _NT) for hh in hs]
        u = [_split2(ws[hh][:C] + wu[hh][:, hd:]) for hh in hs]
        s_add = [_dot3(catp([u[hh], v_h[hh]], 0), bk[hh], _TN) for hh in hs]
        for hh in hs:
            s_ref[heads_per_step * p + hh] = (s0[hh] + s_add[hh]) * cut(g_end, hh)
        ys = [ws[hh][C:] + _dot3(m_rb[hh], u[hh]) + av[hh][C:] for hh in hs]
        y_ref[0, 0, :, sl] = jnp.concatenate(ys, axis=1)
        return carry

    lax.fori_loop(0, lw_ref.shape[-1] // SCAN_LANES, head_group, 0)


def _wkv_scan(r, v, kk, log_decay, k_dir, kka, n_ctx):
    B_, T, D = r.shape
    C = SCAN_CHUNK
    assert n_ctx % C == 0 and T % C == 0 and D % SCAN_LANES == 0
    nc_ctx, nc = n_ctx // C, T // C

    def chunk(d, j):
        back = jnp.where(j < nc_ctx, nc_ctx - 1 - j, nc + nc_ctx - 1 - j)
        return jnp.where(d == 0, j, back)

    shared = pl.BlockSpec((1, C, D), lambda d, b, j: (b, chunk(d, j), 0))
    directed = pl.BlockSpec((1, 1, C, D), lambda d, b, j: (d, b, chunk(d, j), 0))
    return pl.pallas_call(
        _scan_kernel,
        grid=(2, B_, nc),
        in_specs=[shared, shared, shared, directed, directed, directed],
        out_specs=directed,
        out_shape=jax.ShapeDtypeStruct((2, B_, T, D), F32),
        scratch_shapes=[pltpu.VMEM((D // RW_HEAD_DIM, RW_HEAD_DIM, RW_HEAD_DIM), F32)],
        compiler_params=pltpu.CompilerParams(dimension_semantics=("parallel", "parallel", "arbitrary")),
        name="wkv_scan",
    )(r, v, kk, log_decay, k_dir, kka)


def _heads(t):
    return t.astype(F32).reshape(t.shape[:-1] + (t.shape[-1] // RW_HEAD_DIM, RW_HEAD_DIM))


def _rwkv_mixer(hc, hl, prep_params, g1, g2, r_k, gn_g, gn_b, w_o):
    pc = _rwkv_prep(hc, *prep_params)
    pl_ = _rwkv_prep(hl, *prep_params)
    n_ctx = hc.shape[1]

    def seq(name):
        return jnp.concatenate([pc[name], pl_[name]], axis=-2)

    y = _wkv_scan(seq("r"), seq("v"), seq("kk"), seq("log_decay"), seq("k"), seq("kka"), n_ctx)
    p, h = pl_, hl
    wkv = _heads(y[0, :, n_ctx:] + y[1, :, n_ctx:])
    m = jnp.mean(wkv, -1, keepdims=True)
    var = jnp.mean(jnp.square(wkv - m), -1, keepdims=True)
    gn = (wkv - m) * lax.rsqrt(var + RW_GN_EPS) * _heads(gn_g) + _heads(gn_b)
    bonus = jnp.sum(_heads(p["r"])[None] * _heads(p["k"]) * _heads(r_k), axis=(0, -1))[..., None] * _heads(p["v"])
    g = jax.nn.sigmoid(p["xg"] @ g1) @ g2
    o = (gn + bonus).reshape(h.shape).astype(h.dtype) * g
    return o @ w_o


@functools.lru_cache(maxsize=None)
def _fnet_dft_constants(n1, group):
    n = n1 * DFT_MINOR
    ang = 2 * np.pi * np.outer(np.arange(n1), np.arange(n1)) / n1
    c1, s1 = np.cos(ang), np.sin(ang)
    fwd = np.kron(np.block([[c1, s1], [-s1, c1]]), np.eye(SUBLANES))
    phi = 2 * np.pi * np.outer(np.arange(n1), np.arange(DFT_MINOR)) / n
    tw_c = jnp.asarray(np.repeat(np.cos(phi)[:, :, None], LANE_TILE, axis=2), F32)
    tw_s = jnp.asarray(np.repeat(np.sin(phi)[:, :, None], LANE_TILE, axis=2), F32)
    a2 = 2 * np.pi * np.outer(np.arange(DFT_MINOR), np.arange(DFT_MINOR)) / DFT_MINOR
    minor_re = np.concatenate([np.cos(a2), np.sin(a2)], axis=1) / math.sqrt(n * group)
    ag = 2 * np.pi * np.outer(np.arange(group), np.arange(group)) / group
    lane = np.concatenate([np.cos(ag), -np.sin(ag)], axis=1)
    lane_hi = lane.astype(ml_dtypes.bfloat16)
    lane_lo = (lane - lane_hi.astype(np.float64)).astype(ml_dtypes.bfloat16)
    lane_parts = jnp.asarray(np.concatenate([lane_hi, lane_hi, lane_lo], axis=0))
    return dict(nk=n1, fwd=_lhs_parts(fwd), tw_c=tw_c, tw_s=tw_s, minor_re=_lhs_parts(minor_re), lane=lane_parts)


def _lane_dft_kernel(f_ref, x_ref, o_ref):
    group = f_ref.shape[1] // 2
    for g in range(x_ref.shape[-1] // group):
        ls = slice(g * group, (g + 1) * group)
        z = jnp.dot(_rhs_parts_t(x_ref[:, ls]), f_ref[...], preferred_element_type=F32)
        o_ref[0, :, ls] = z[:, :group]
        o_ref[1, :, ls] = z[:, group:]


def _lane_dft(x, consts, group):
    B_, T, D = x.shape
    f = consts["lane"]
    return pl.pallas_call(
        _lane_dft_kernel,
        grid=(B_, T // FN_ROWS),
        in_specs=[pl.BlockSpec(f.shape, lambda b, t: (0, 0)),
                  pl.BlockSpec((None, FN_ROWS, D), lambda b, t: (b, t, 0))],
        out_specs=pl.BlockSpec((None, 2, FN_ROWS, D), lambda b, t: (b, 0, t, 0)),
        out_shape=jax.ShapeDtypeStruct((B_, 2, T, D), F32),
        compiler_params=pltpu.CompilerParams(dimension_semantics=("parallel", "parallel")),
        name="fnet_lane_dft",
    )(f, x)


def _dft_minor_real_kernel(m_ref, a_ref, o_ref):
    for q in range(SUBLANES):
        a = jnp.concatenate([a_ref[0, q], a_ref[1, q]], axis=0)
        o_ref[:, q, :] = jnp.dot(m_ref[...], _rhs_parts(a), preferred_element_type=F32)


def _dft_minor_real(a, consts):
    n, _, n1, _, width = a.shape
    m = consts["minor_re"]
    out = pl.pallas_call(
        _dft_minor_real_kernel,
        grid=(n1 // SUBLANES, width // DFT_DC, n),
        in_specs=[pl.BlockSpec(m.shape, lambda k, j, b: (0, 0)),
                  pl.BlockSpec((None, 2, SUBLANES, DFT_MINOR, DFT_DC), lambda k, j, b: (b, 0, k, 0, j))],
        out_specs=pl.BlockSpec((None, DFT_MINOR, SUBLANES, DFT_DC), lambda k, j, b: (b, 0, k, j)),
        out_shape=jax.ShapeDtypeStruct((n, DFT_MINOR, n1, width), F32),
        compiler_params=pltpu.CompilerParams(dimension_semantics=("parallel", "parallel", "parallel")),
        name="dft_minor_real",
    )(m, a)
    return out.reshape(n, DFT_MINOR * n1, width)


def _fourier(h, w_o, b_o):
    B_, T, D = h.shape
    n1 = T // DFT_MINOR
    group = D // FN_GROUPS
    consts = _fnet_dft_constants(n1, group)
    z = _lane_dft(h.astype(F32), consts, group).reshape(B_, 2 * T, D)
    a = _dft_major_fwd(z, B_, lambda b: (b, 0), 2 * n1, consts, D)
    mixed = _dft_minor_real(a, consts)
    return mixed.astype(h.dtype) @ w_o + b_o


def _moe_ffn_kernel(x_ref, g_ref, w1_ref, w3_ref, w2_ref, o_ref, xb_ref):
    f = pl.program_id(2)

    @pl.when(f == 0)
    def _():
        xb_ref[...] = x_ref[...].reshape(xb_ref.shape).astype(BF16)

    xb = xb_ref[...]
    h1 = jnp.dot(xb, w1_ref[...].astype(BF16), preferred_element_type=F32)
    h3 = jnp.dot(xb, w3_ref[...].astype(BF16), preferred_element_type=F32)
    he = (h1 * jax.nn.sigmoid(h1) * h3).astype(BF16)
    part = jnp.dot(he, w2_ref[...].astype(BF16), preferred_element_type=F32).reshape(o_ref.shape)

    @pl.when(f == 0)
    def _():
        o_ref[...] = part

    @pl.when(f > 0)
    def _():
        o_ref[...] += part

    @pl.when(f == pl.num_programs(2) - 1)
    def _():
        o_ref[...] *= g_ref[...]


def _moe_ffn(xe, gate, w1, w3, w2):
    B_, E, cap, D = xe.shape
    F = w1.shape[-1]
    mb = max(1, min(B_, MOE_ROWS // cap))
    ft = min(F, MOE_F_TILE)
    assert B_ % mb == 0 and F % ft == 0
    tok = pl.BlockSpec((mb, None, cap, D), lambda e, m, f: (m, e, 0, 0))
    return pl.pallas_call(
        _moe_ffn_kernel,
        grid=(E, B_ // mb, F // ft),
        in_specs=[tok,
                  pl.BlockSpec((mb, None, cap, 1), lambda e, m, f: (m, e, 0, 0)),
                  pl.BlockSpec((None, D, ft), lambda e, m, f: (e, 0, f)),
                  pl.BlockSpec((None, D, ft), lambda e, m, f: (e, 0, f)),
                  pl.BlockSpec((None, ft, D), lambda e, m, f: (e, f, 0))],
        out_specs=tok,
        out_shape=jax.ShapeDtypeStruct((B_, E, cap, D), F32),
        scratch_shapes=[pltpu.VMEM((mb * cap, D), BF16)],
        compiler_params=pltpu.CompilerParams(dimension_semantics=("parallel", "parallel", "arbitrary"),
                                             vmem_limit_bytes=MOE_VMEM_BYTES),
        name="moe_ffn",
    )(xe, gate[..., None].astype(F32), w1, w3, w2)


def _moe_scatter_kernel(idx_ref, ye_ref, buf_ref, o_ref):
    del buf_ref
    e = pl.program_id(0)
    cap = ye_ref.shape[0]

    @pl.when(e == 0)
    def _():
        o_ref[...] = jnp.zeros_like(o_ref)

    def group(g, carry):
        rows = []
        for u in range(MOE_SCATTER_UNROLL):
            j = g * MOE_SCATTER_UNROLL + u
            t = idx_ref[e * cap + j]
            rows.append((t, o_ref[pl.ds(t, 1), :] + ye_ref[pl.ds(j, 1), :]))
        for t, r in rows:
            o_ref[pl.ds(t, 1), :] = r
        return carry

    lax.fori_loop(0, cap // MOE_SCATTER_UNROLL, group, 0)


def _moe_scatter(ye, idx, T):
    B_, E, cap, D = ye.shape
    assert cap % MOE_SCATTER_UNROLL == 0
    out = jnp.zeros((B_, T, D), F32)
    for b in range(B_):
        out = pl.pallas_call(
            _moe_scatter_kernel,
            grid_spec=pltpu.PrefetchScalarGridSpec(
                num_scalar_prefetch=1,
                grid=(E,),
                in_specs=[pl.BlockSpec((None, None, cap, D), lambda e, idx_ref, b=b: (b, e, 0, 0)),
                          pl.BlockSpec(memory_space=pl.ANY)],
                out_specs=pl.BlockSpec((None, T, D), lambda e, idx_ref, b=b: (b, 0, 0)),
            ),
            out_shape=jax.ShapeDtypeStruct((B_, T, D), F32),
            input_output_aliases={2: 0},
            compiler_params=pltpu.CompilerParams(dimension_semantics=("arbitrary",),
                                                 vmem_limit_bytes=MOE_VMEM_BYTES),
            name="moe_scatter",
        )(idx[b].reshape(E * cap).astype(jnp.int32), ye, out)
    return out


def _expert_choice_moe(h, w_router, w1, w3, w2):
    B_, T, _ = h.shape
    cap = EC_CAPACITY * T // N_EXPERTS
    aff = jax.nn.softmax((h @ w_router).astype(F32), -1)
    gate, idx = lax.top_k(jnp.swapaxes(aff, 1, 2), cap)
    bidx = jnp.arange(B_)[:, None, None]
    xe = h[bidx, idx]
    ye = _moe_ffn(xe, gate, w1, w3, w2)
    return _moe_scatter(ye, idx, T).astype(h.dtype)


def kernel(x, c, ctx, c_ctx, mod_w, mod_b, ln_g, ln_b, moe_router, moe_w1, moe_w3, moe_w2, hy_w_in, hy_b_in, hy_conv_w, hy_conv_b, hy_f_w1, hy_f_b1, hy_f_freq, hy_f_w2, hy_f_b2, hy_f_w3, hy_f_bias, hy_w_out, hy_b_out, rw_mu, rw_wr, rw_wk, rw_wv, rw_w0, rw_w1, rw_w2, rw_a0, rw_a1, rw_a2, rw_kk, rw_ka, rw_g1, rw_g2, rw_rk, rw_gn_g, rw_gn_b, rw_wo, fn_wo, fn_bo):
    depth = mod_w.shape[0]
    D = x.shape[-1]
    alpha = (2 * depth) ** 0.25
    n_lat = x.shape[1]
    x = x + _grid_pos_embed(n_lat // GRID_W, D).astype(x.dtype)[None]
    xc = ctx
    readers = [i for i in range(depth) if i % N_MIXERS == 1]
    last_reader = readers[-1] if readers else -1
    for i in range(depth):
        kind, j = i % N_MIXERS, i // N_MIXERS
        ctx_in = i <= last_reader
        ctx_full = i < last_reader
        m_l = jnp.split((jax.nn.silu(c) @ mod_w[i] + mod_b[i])[:, None, :], 6, -1)
        hl = _modulate(x, m_l[0], m_l[1])
        if ctx_in:
            m_c = jnp.split((jax.nn.silu(c_ctx) @ mod_w[i] + mod_b[i])[None, None, :], 6, -1)
            hc = _modulate(xc, m_c[0], m_c[1])
        if kind == 0:
            hy = (hy_w_in[j], hy_b_in[j], hy_conv_w[j], hy_conv_b[j], hy_f_w1[j], hy_f_b1[j], hy_f_freq[j],
                  hy_f_w2[j], hy_f_b2[j], hy_f_w3[j], hy_f_bias[j], hy_w_out[j], hy_b_out[j])
            yl = _hyena(hl, *hy)
            yc = _hyena(hc, *hy) if ctx_full else None
        elif kind == 1:
            prep = (rw_mu[j], rw_wr[j], rw_wk[j], rw_wv[j], rw_w0[j], rw_w1[j], rw_w2[j],
                    rw_a0[j], rw_a1[j], rw_a2[j], rw_kk[j], rw_ka[j])
            assert not ctx_full
            yc = None
            yl = _rwkv_mixer(hc, hl, prep, rw_g1[j], rw_g2[j], rw_rk[j], rw_gn_g[j], rw_gn_b[j], rw_wo[j])
        else:
            yl = _fourier(hl, fn_wo[j], fn_bo[j])
            yc = _fourier(hc, fn_wo[j], fn_bo[j]) if ctx_full else None
        x = _ln_affine(alpha * x + (1.0 + m_l[2]) * yl, ln_g[i, 0], ln_b[i, 0])
        hl = _modulate(x, m_l[3], m_l[4])
        x = _ln_affine(alpha * x + (1.0 + m_l[5]) * _expert_choice_moe(hl, moe_router[i], moe_w1[i], moe_w3[i], moe_w2[i]),
                       ln_g[i, 1], ln_b[i, 1])
        if ctx_full:
            xc = _ln_affine(alpha * xc + (1.0 + m_c[2]) * yc, ln_g[i, 0], ln_b[i, 0])
            hc = _modulate(xc, m_c[3], m_c[4])
            xc = _ln_affine(alpha * xc + (1.0 + m_c[5]) * _expert_choice_moe(hc, moe_router[i], moe_w1[i], moe_w3[i], moe_w2[i]),
                            ln_g[i, 1], ln_b[i, 1])
    return x
```
